```python
import math
import jax, jax.numpy as jnp
from jax import lax
import numpy as np

D_MODEL = 2048
BATCH = 16
SEQ = 2048
DEPTH = 4

N_BRANCH = 4
BRANCH_WIDTH = D_MODEL // 4
ROPE_THETA = 500000.0
Q_BLOCK = 128
NORM_EPS = 1e-5

DIFF_HEADS = 4
DIFF_HEAD_DIM = BRANCH_WIDTH // (2 * DIFF_HEADS)
DIFF_V_DIM = 2 * DIFF_HEAD_DIM
DIFF_ROT = DIFF_HEAD_DIM // 4

GLA_HEADS = 4
GLA_DK = BRANCH_WIDTH // 2 // GLA_HEADS
GLA_DV = BRANCH_WIDTH // GLA_HEADS
GLA_GATE_RANK = 16
GLA_TAU = 16.0
GLA_CHUNK = 64

MLA_HEADS = 4
MLA_NOPE = 128
MLA_ROPE = 64
MLA_V = BRANCH_WIDTH // MLA_HEADS
MLA_Q_RANK = 384
MLA_KV_RANK = 128

SSD_HEADDIM = 64
SSD_INNER = BRANCH_WIDTH
SSD_HEADS = SSD_INNER // SSD_HEADDIM
SSD_GROUPS = 2
SSD_STATE = 128
SSD_CONV = 4
SSD_CONV_DIM = SSD_INNER + 2 * SSD_GROUPS * SSD_STATE
SSD_CHUNK = 128

D_FF = 256 * ((8 * D_MODEL // 3 + 255) // 256)

ALPHA = (2 * DEPTH) ** 0.25
BETA = (8 * DEPTH) ** -0.25

IN_SPLITS = [
    DIFF_HEADS * 2 * DIFF_HEAD_DIM, DIFF_HEADS * 2 * DIFF_HEAD_DIM, DIFF_HEADS * DIFF_V_DIM,
    GLA_HEADS * GLA_DK, GLA_HEADS * GLA_DK, GLA_HEADS * GLA_DV, GLA_GATE_RANK, BRANCH_WIDTH,
    MLA_Q_RANK, MLA_KV_RANK, MLA_ROPE,
    SSD_INNER, SSD_CONV_DIM, SSD_HEADS,
    N_BRANCH * D_MODEL,
]
N_IN = sum(IN_SPLITS)

kernel_name = "hybrid_diff_gla_mla_ssd_macaron_deepnorm"


def split_cols(t, sizes):
    return jnp.split(t, [int(i) for i in np.cumsum(sizes)[:-1]], axis=-1)


def layer_norm(x, g, b):
    xf = x.astype(jnp.float32)
    mu = jnp.mean(xf, axis=-1, keepdims=True)
    xc = xf - mu
    var = jnp.mean(xc * xc, axis=-1, keepdims=True)
    return (xc * lax.rsqrt(var + NORM_EPS) * g + b).astype(x.dtype)


def rms_norm(x, g):
    xf = x.astype(jnp.float32)
    return (xf * lax.rsqrt(jnp.mean(xf * xf, axis=-1, keepdims=True) + NORM_EPS) * g).astype(x.dtype)


def swiglu(h, w_gu, w_down):
    gate, up = jnp.split(h @ w_gu, 2, axis=-1)
    return (jax.nn.silu(gate) * up) @ w_down


def rope_tables(positions, rot_dim):
    inv = ROPE_THETA ** (-jnp.arange(0, rot_dim, 2, dtype=jnp.float32) / rot_dim)
    ang = positions.astype(jnp.float32)[..., None] * inv
    return jnp.cos(ang), jnp.sin(ang)


def apply_rope(x, cos, sin):
    x1, x2 = jnp.split(x, 2, axis=-1)
    return jnp.concatenate([x1 * cos - x2 * sin, x1 * sin + x2 * cos], axis=-1).astype(x.dtype)


def partial_rope(x, cos, sin, rot):
    return jnp.concatenate([apply_rope(x[..., :rot], cos, sin), x[..., rot:]], axis=-1)


def causal_softmax_attend(q, k, v, scale):
    bsz, m, nh, s, dk = q.shape
    nb = s // Q_BLOCK
    q_blocks = jnp.moveaxis(q.reshape(bsz, m, nh, nb, Q_BLOCK, dk), 3, 0)
    key_pos = jnp.arange(s)

    def one_block(args):
        qb, i = args
        sc = jnp.einsum('bmhqd,bmhkd->bmhqk', qb, k).astype(jnp.float32) * scale
        q_pos = i * Q_BLOCK + jnp.arange(Q_BLOCK)
        sc = jnp.where(q_pos[:, None] >= key_pos[None, :], sc, -jnp.inf)
        p = jax.nn.softmax(sc, axis=-1).astype(v.dtype)
        return jnp.einsum('bmhqk,bhkd->bmhqd', p, v)

    o = lax.map(one_block, (q_blocks, jnp.arange(nb)))
    return jnp.moveaxis(o, 0, 3).reshape(bsz, m, nh, s, v.shape[-1])


def chunk_scan(decay, inc):
    def step(state, xs):
        d, u = xs
        return (d * state + u).astype(state.dtype), state
    _, prev = lax.scan(step, jnp.zeros_like(inc[0]), (decay, inc))
    return prev


def diff_attention_branch(q, k, v, lam_params, subln_g, cos, sin, layer):
    bsz, s, _ = q.shape
    q = q.reshape(bsz, s, DIFF_HEADS, 2, DIFF_HEAD_DIM).transpose(0, 3, 2, 1, 4)
    k = k.reshape(bsz, s, DIFF_HEADS, 2, DIFF_HEAD_DIM).transpose(0, 3, 2, 1, 4)
    c, sn = cos[:, None, None], sin[:, None, None]
    q = partial_rope(q, c, sn, DIFF_ROT)
    k = partial_rope(k, c, sn, DIFF_ROT)
    v = v.reshape(bsz, s, DIFF_HEADS, DIFF_V_DIM).transpose(0, 2, 1, 3)
    o = causal_softmax_attend(q, k, v, DIFF_HEAD_DIM ** -0.5)
    lam_init = 0.8 - 0.6 * math.exp(-0.3 * layer)
    lp = lam_params.astype(jnp.float32)
    lam = jnp.exp(jnp.sum(lp[0] * lp[1])) - jnp.exp(jnp.sum(lp[2] * lp[3])) + lam_init
    o = o[:, 0] - lam.astype(o.dtype) * o[:, 1]
    o = rms_norm(o, subln_g) * (1.0 - lam_init)
    return o.transpose(0, 2, 1, 3).reshape(bsz, s, BRANCH_WIDTH)


def gla_branch(q, k, v, g_low, r, w_gate2, b_gate, norm_g):
    bsz, s, _ = q.shape
    nc = s // GLA_CHUNK

    def heads(t, d):
        return t.reshape(bsz, s, GLA_HEADS, d).transpose(0, 2, 1, 3).reshape(bsz, GLA_HEADS, nc, GLA_CHUNK, d)

    g = jax.nn.log_sigmoid((g_low @ w_gate2 + b_gate).astype(jnp.float32)) / GLA_TAU
    qc = heads(q, GLA_DK) * GLA_DK ** -0.5
    kc = heads(k, GLA_DK)
    vc = heads(v, GLA_DV)
    b = jnp.cumsum(heads(g, GLA_DK), axis=3)
    b_last = b[:, :, :, -1:, :]
    q_t = qc * jnp.exp(b)
    k_t = kc * jnp.exp(-b)
    mask = jnp.tril(jnp.ones((GLA_CHUNK, GLA_CHUNK), dtype=bool))
    att = jnp.where(mask, jnp.einsum('bhnid,bhnjd->bhnij', q_t, k_t), 0.0)
    o_intra = jnp.einsum('bhnij,bhnje->bhnie', att, vc)
    inc = jnp.einsum('bhnjd,bhnje->bhnde', kc * jnp.exp(b_last - b), vc)
    decay = jnp.swapaxes(jnp.exp(b_last), -1, -2)
    prev = jnp.moveaxis(chunk_scan(jnp.moveaxis(decay, 2, 0), jnp.moveaxis(inc, 2, 0)), 0, 2)
    o = o_intra + jnp.einsum('bhnid,bhnde->bhnie', q_t, prev)
    o = rms_norm(o, norm_g).reshape(bsz, GLA_HEADS, s, GLA_DV).transpose(0, 2, 1, 3).reshape(bsz, s, BRANCH_WIDTH)
    return (o * jax.nn.silu(r)).astype(q.dtype)


def mla_branch(c_q, c_kv, k_r, cos, sin, q_norm_g, w_uq, kv_norm_g, w_ukv):
    bsz, s, _ = c_q.shape
    q = (rms_norm(c_q, q_norm_g) @ w_uq).reshape(bsz, s, MLA_HEADS, MLA_NOPE + MLA_ROPE)
    q = jnp.concatenate([q[..., :MLA_NOPE], apply_rope(q[..., MLA_NOPE:], cos[:, :, None], sin[:, :, None])], axis=-1)
    kv = (rms_norm(c_kv, kv_norm_g) @ w_ukv).reshape(bsz, s, MLA_HEADS, MLA_NOPE + MLA_V)
    k_nope, v = kv[..., :MLA_NOPE], kv[..., MLA_NOPE:]
    k_rope = jnp.broadcast_to(apply_rope(k_r, cos, sin)[:, :, None, :], (bsz, s, MLA_HEADS, MLA_ROPE))
    k = jnp.concatenate([k_nope, k_rope], axis=-1)
    o = causal_softmax_attend(q.transpose(0, 2, 1, 3)[:, None], k.transpose(0, 2, 1, 3)[:, None],
                              v.transpose(0, 2, 1, 3), (MLA_NOPE + MLA_ROPE) ** -0.5)[:, 0]
    return o.transpose(0, 2, 1, 3).reshape(bsz, s, BRANCH_WIDTH)


def causal_depthwise_conv(x, w, b):
    kw, ch = w.shape
    out = lax.conv_general_dilated(x, w[:, None, :].astype(x.dtype), window_strides=(1,), padding=[(kw - 1, 0)],
                                   dimension_numbers=('NWC', 'WIO', 'NWC'), feature_group_count=ch)
    return out + b


def ssd_chunked(x, dt, a, bm, cm):
    bsz, s, nh, p = x.shape
    g, n = bm.shape[-2:]
    hg = nh // g
    nc = s // SSD_CHUNK
    xdt = (x * dt[..., None]).reshape(bsz, nc, SSD_CHUNK, g, hg, p)
    a_cum = jnp.cumsum((dt * a).reshape(bsz, nc, SSD_CHUNK, g, hg), axis=2)
    bc = bm.reshape(bsz, nc, SSD_CHUNK, g, n)
    cc = cm.reshape(bsz, nc, SSD_CHUNK, g, n)
    mask = jnp.tril(jnp.ones((SSD_CHUNK, SSD_CHUNK), dtype=bool))[:, :, None, None]
    seg = a_cum[:, :, :, None] - a_cum[:, :, None, :]
    lmat = jnp.exp(jnp.where(mask, seg, -jnp.inf))
    cb = jnp.einsum('bnigs,bnjgs->bnijg', cc, bc)
    y_diag = jnp.einsum('bnijg,bnijgh,bnjghp->bnighp', cb, lmat, xdt)
    decay_states = jnp.exp(a_cum[:, :, -1:] - a_cum)
    inc = jnp.einsum('bnjgs,bnjgh,bnjghp->bnghps', bc, decay_states, xdt)
    chunk_decay = jnp.exp(a_cum[:, :, -1])[..., None, None]
    prev = jnp.moveaxis(chunk_scan(jnp.moveaxis(chunk_decay, 1, 0), jnp.moveaxis(inc, 1, 0)), 0, 1)
    y_off = jnp.einsum('bnigs,bnghps,bnigh->bnighp', cc, prev, jnp.exp(a_cum))
    return (y_diag + y_off).reshape(bsz, s, nh, p)


def ssd_branch(z, xbc, dt, conv_w, conv_b, dt_bias, a_log, d_skip, norm_g):
    bsz, s, _ = z.shape
    xbc = jax.nn.silu(causal_depthwise_conv(xbc, conv_w, conv_b))
    xs, bm, cm = split_cols(xbc, [SSD_INNER, SSD_GROUPS * SSD_STATE, SSD_GROUPS * SSD_STATE])
    xs = xs.reshape(bsz, s, SSD_HEADS, SSD_HEADDIM)
    dt = jax.nn.softplus((dt + dt_bias).astype(jnp.float32))
    a = -jnp.exp(a_log.astype(jnp.float32))
    y = ssd_chunked(xs, dt, a, bm.reshape(bsz, s, SSD_GROUPS, SSD_STATE), cm.reshape(bsz, s, SSD_GROUPS, SSD_STATE))
    y = y + d_skip[:, None] * xs
    y = (y.reshape(bsz, s, SSD_INNER) * jax.nn.silu(z)).reshape(bsz, s, SSD_GROUPS, SSD_INNER // SSD_GROUPS)
    y = rms_norm(y, norm_g.reshape(SSD_GROUPS, SSD_INNER // SSD_GROUPS)).reshape(bsz, s, SSD_INNER)
    return y.astype(z.dtype)


def hybrid_mixer(h, layer, cos_d, sin_d, cos_m, sin_m, w_in, b_gate, diff_lambda, diff_subln_g,
                 gla_w_gate2, gla_b_gate, gla_norm_g, mla_q_norm_g, mla_w_uq, mla_kv_norm_g, mla_w_ukv,
                 ssd_conv_w, ssd_conv_b, ssd_dt_bias, ssd_a_log, ssd_d, ssd_norm_g, w_branch, w_out):
    bsz, s, _ = h.shape
    (a_q, a_k, a_v, b_q, b_k, b_v, b_glow, b_r, c_q, c_kv, c_kr,
     d_z, d_xbc, d_dt, gate_logits) = split_cols(h @ w_in, IN_SPLITS)
    o_a = diff_attention_branch(a_q, a_k, a_v, diff_lambda, diff_subln_g, cos_d, sin_d, layer)
    o_b = gla_branch(b_q, b_k, b_v, b_glow, b_r, gla_w_gate2, gla_b_gate, gla_norm_g)
    o_c = mla_branch(c_q, c_kv, c_kr, cos_m, sin_m, mla_q_norm_g, mla_w_uq, mla_kv_norm_g, mla_w_ukv)
    o_d = ssd_branch(d_z, d_xbc, d_dt, ssd_conv_w, ssd_conv_b, ssd_dt_bias, ssd_a_log, ssd_d, ssd_norm_g)
    gates = jax.nn.sigmoid((gate_logits + b_gate).astype(jnp.float32)).astype(h.dtype)
    gates = gates.reshape(bsz, s, N_BRANCH, D_MODEL)
    outs = (o_a, o_b, o_c, o_d)
    merged = gates[:, :, 0] * (o_a @ w_branch[0])
    for i in range(1, N_BRANCH):
        merged = merged + gates[:, :, i] * (outs[i] @ w_branch[i])
    return merged @ w_out


def setup_inputs(seed: int = 0) -> dict:
    key = jax.random.key(seed)
    ks = jax.random.split(key, 32)
    f32 = jnp.float32

    def nrm(k, shape, fan_in, gain=1.0):
        return jax.random.normal(k, shape, f32) * (gain * fan_in ** -0.5)

    def gain(k, shape):
        return 1.0 + 0.02 * jax.random.normal(k, shape, f32)

    def small(k, shape):
        return 0.02 * jax.random.normal(k, shape, f32)

    x = jax.random.normal(ks[0], (BATCH, SEQ, D_MODEL), f32)
    positions = (jax.random.randint(ks[1], (BATCH, 1), 0, 4096) + jnp.arange(SEQ)[None, :]).astype(jnp.int32)
    dt0 = jnp.exp(jax.random.uniform(ks[22], (DEPTH, SSD_HEADS), f32, math.log(1e-3), math.log(1e-1)))
    return {
        "x": x,
        "positions": positions,
        "ln_g": gain(ks[2], (DEPTH, 3, D_MODEL)),
        "ln_b": small(ks[3], (DEPTH, 3, D_MODEL)),
        "ffn1_w_gu": nrm(ks[4], (DEPTH, D_MODEL, 2 * D_FF), D_MODEL),
        "ffn1_w_down": nrm(ks[5], (DEPTH, D_FF, D_MODEL), D_FF, BETA),
        "ffn2_w_gu": nrm(ks[6], (DEPTH, D_MODEL, 2 * D_FF), D_MODEL),
        "ffn2_w_down": nrm(ks[7], (DEPTH, D_FF, D_MODEL), D_FF, BETA),
        "w_in": nrm(ks[8], (DEPTH, D_MODEL, N_IN), D_MODEL),
        "b_gate": small(ks[9], (DEPTH, N_BRANCH * D_MODEL)),
        "diff_lambda": 0.1 * jax.random.normal(ks[10], (DEPTH, 4, DIFF_HEAD_DIM), f32),
        "diff_subln_g": gain(ks[11], (DEPTH, DIFF_V_DIM)),
        "gla_w_gate2": nrm(ks[12], (DEPTH, GLA_GATE_RANK, GLA_HEADS * GLA_DK), GLA_GATE_RANK),
        "gla_b_gate": small(ks[13], (DEPTH, GLA_HEADS * GLA_DK)),
        "gla_norm_g": gain(ks[14], (DEPTH, GLA_DV)),
        "mla_q_norm_g": gain(ks[15], (DEPTH, MLA_Q_RANK)),
        "mla_w_uq": nrm(ks[16], (DEPTH, MLA_Q_RANK, MLA_HEADS * (MLA_NOPE + MLA_ROPE)), MLA_Q_RANK),
        "mla_kv_norm_g": gain(ks[17], (DEPTH, MLA_KV_RANK)),
        "mla_w_ukv": nrm(ks[18], (DEPTH, MLA_KV_RANK, MLA_HEADS * (MLA_NOPE + MLA_V)), MLA_KV_RANK),
        "ssd_conv_w": jax.random.uniform(ks[19], (DEPTH, SSD_CONV, SSD_CONV_DIM), f32, -SSD_CONV ** -0.5, SSD_CONV ** -0.5),
        "ssd_conv_b": small(ks[20], (DEPTH, SSD_CONV_DIM)),
        "ssd_dt_bias": dt0 + jnp.log(-jnp.expm1(-dt0)),
        "ssd_a_log": jnp.log(jax.random.uniform(ks[23], (DEPTH, SSD_HEADS), f32, 1.0, 16.0)),
        "ssd_d": gain(ks[24], (DEPTH, SSD_HEADS)),
        "ssd_norm_g": gain(ks[25], (DEPTH, SSD_INNER)),
        "w_branch": nrm(ks[26], (DEPTH, N_BRANCH, BRANCH_WIDTH, D_MODEL), BRANCH_WIDTH),
        "w_out": nrm(ks[27], (DEPTH, D_MODEL, D_MODEL), D_MODEL, BETA),
    }


def reference(x, positions, ln_g, ln_b, ffn1_w_gu, ffn1_w_down, ffn2_w_gu, ffn2_w_down, w_in, b_gate,
              diff_lambda, diff_subln_g, gla_w_gate2, gla_b_gate, gla_norm_g, mla_q_norm_g, mla_w_uq,
              mla_kv_norm_g, mla_w_ukv, ssd_conv_w, ssd_conv_b, ssd_dt_bias, ssd_a_log, ssd_d, ssd_norm_g,
              w_branch, w_out):
    cos_d, sin_d = rope_tables(positions, DIFF_ROT)
    cos_m, sin_m = rope_tables(positions, MLA_ROPE)
    h = x
    for l in range(DEPTH):
        h = layer_norm(ALPHA * h + 0.5 * swiglu(h, ffn1_w_gu[l], ffn1_w_down[l]), ln_g[l, 0], ln_b[l, 0])
        mix = hybrid_mixer(h, l, cos_d, sin_d, cos_m, sin_m, w_in[l], b_gate[l], diff_lambda[l], diff_subln_g[l],
                           gla_w_gate2[l], gla_b_gate[l], gla_norm_g[l], mla_q_norm_g[l], mla_w_uq[l],
                           mla_kv_norm_g[l], mla_w_ukv[l], ssd_conv_w[l], ssd_conv_b[l], ssd_dt_bias[l],
                           ssd_a_log[l], ssd_d[l], ssd_norm_g[l], w_branch[l], w_out[l])
        h = layer_norm(ALPHA * h + mix, ln_g[l, 1], ln_b[l, 1])
        h = layer_norm(ALPHA * h + 0.5 * swiglu(h, ffn2_w_gu[l], ffn2_w_down[l]), ln_g[l, 2], ln_b[l, 2])
    return h
```

```python
import functools
import math

import jax
import jax.numpy as jnp
from jax import lax
from jax.experimental import pallas as pl
from jax.experimental.pallas import tpu as pltpu

F32 = jnp.float32
BF16 = jnp.bfloat16

D_MODEL = 2048
N_BRANCH = 4
BRANCH_WIDTH = D_MODEL // 4
ROPE_THETA = 500000.0
NORM_EPS = 1e-5

DIFF_HEADS = 4
DIFF_HEAD_DIM = 64
DIFF_ROT = 16

GLA_HEADS = 4
GLA_DK = 64
GLA_DV = 128
GLA_GATE_RANK = 16
GLA_TAU = 16.0
GLA_CHUNK = 64

MLA_HEADS = 4
MLA_NOPE = 128
MLA_ROPE = 64
MLA_V = 128
MLA_Q_RANK = 384
MLA_KV_RANK = 128

SSD_HEADDIM = 64
SSD_INNER = 512
SSD_HEADS = 8
SSD_GROUPS = 2
SSD_STATE = 128
SSD_CONV = 4
SSD_CHUNK = 128

D_FF = 5632

IN_SPLITS = [512, 512, 512, 256, 256, 512, 16, 512, 384, 128, 64, 512, 1024, 8, 4 * D_MODEL]

LANES = 128
VMEM_LIMIT_BYTES = 56 * 2**20

MIX_CQ = 0
MIX_CKV = 3
MIX_AQ = 4
MIX_AK = 8
MIX_AV = 12
MIX_BQ = 16
MIX_BK = 20
MIX_BV = 24
MIX_BR = 28
MIX_SBC = 32
MIX_SX = 36
MIX_SZ = 40
MIX_CKR = 44
MIX_WIDTH = 45 * LANES
SMALL_GLOW = 0
SMALL_DT = 16

NEG_BIG = -1e30


def _cparams(semantics):
    return pltpu.CompilerParams(dimension_semantics=semantics, vmem_limit_bytes=VMEM_LIMIT_BYTES)


def _layer_norm(y, g, b):
    mu = jnp.mean(y, axis=-1, keepdims=True)
    yc = y - mu
    var = jnp.mean(yc * yc, axis=-1, keepdims=True)
    return yc * lax.rsqrt(var + NORM_EPS) * g + b


def _rms(x, g):
    return x * lax.rsqrt(jnp.mean(x * x, axis=-1, keepdims=True) + NORM_EPS) * g


def _sigmoid(x):
    return 1.0 / (1.0 + jnp.exp(-x))


def _silu(x):
    return x * _sigmoid(x)


def _softplus(x):
    return jnp.maximum(x, 0.0) + jnp.log(1.0 + jnp.exp(-jnp.abs(x)))


def _dot(a, b):
    return jnp.dot(a, b, preferred_element_type=F32)


def _dot_nt(a, b):
    return lax.dot_general(a, b, (((1,), (1,)), ((), ())), preferred_element_type=F32)


def _dot_tn(a, b):
    return lax.dot_general(a, b, (((0,), (0,)), ((), ())), preferred_element_type=F32)


def _ffn_body(alpha, nj, h_ref, wg_ref, wu_ref, wd_ref, g_ref, b_ref, o_ref, ob_ref, hb_ref, acc_ref):
    j = pl.program_id(1)

    @pl.when(j == 0)
    def _():
        hb_ref[...] = h_ref[...].astype(BF16)

    hb = hb_ref[...]
    gate = _dot(hb, wg_ref[...])
    up = _dot(hb, wu_ref[...])
    act = (_silu(gate) * up).astype(BF16)
    part = _dot(act, wd_ref[...])

    @pl.when(j == 0)
    def _():
        acc_ref[...] = part

    @pl.when(j > 0)
    def _():
        acc_ref[...] += part

    @pl.when(j == nj - 1)
    def _():
        y = alpha * h_ref[...] + 0.5 * acc_ref[...]
        out = _layer_norm(y, g_ref[...], b_ref[...])
        o_ref[...] = out
        ob_ref[...] = out.astype(BF16)


def _ffn(h, w_gu, w_down, ln_g, ln_b, layer, ln_idx, alpha, tm=512, tf=512):
    t, d = h.shape
    dff = w_down.shape[1]
    nj = dff // tf
    grid = (t // tm, nj)
    return pl.pallas_call(
        functools.partial(_ffn_body, alpha, nj),
        grid=grid,
        in_specs=[
            pl.BlockSpec((tm, d), lambda i, j: (i, 0)),
            pl.BlockSpec((None, d, tf), lambda i, j: (layer, 0, j)),
            pl.BlockSpec((None, d, tf), lambda i, j: (layer, 0, nj + j)),
            pl.BlockSpec((None, tf, d), lambda i, j: (layer, j, 0)),
            pl.BlockSpec((None, None, 1, d), lambda i, j: (layer, ln_idx, 0, 0)),
            pl.BlockSpec((None, None, 1, d), lambda i, j: (layer, ln_idx, 0, 0)),
        ],
        out_specs=[
            pl.BlockSpec((tm, d), lambda i, j: (i, 0)),
            pl.BlockSpec((tm, d), lambda i, j: (i, 0)),
        ],
        out_shape=[jax.ShapeDtypeStruct((t, d), F32), jax.ShapeDtypeStruct((t, d), BF16)],
        scratch_shapes=[pltpu.VMEM((tm, d), BF16), pltpu.VMEM((tm, d), F32)],
        compiler_params=_cparams(("parallel", "arbitrary")),
        name="ffn",
    )(h, w_gu, w_gu, w_down, ln_g, ln_b)


def _inproj_body(hb_ref, w_ref, ws_ref, p_ref, ps_ref):
    hb = hb_ref[...]
    p_ref[...] = _dot(hb, w_ref[...]).astype(BF16)

    @pl.when(pl.program_id(1) == 0)
    def _():
        ps_ref[...] = _dot(hb, ws_ref[...])


def _inproj(hb, w_mix, w_small, layer, tm=1024, tn=1920):
    t, d = hb.shape
    n = w_mix.shape[2]
    tm = min(tm, t)
    return pl.pallas_call(
        _inproj_body,
        grid=(t // tm, n // tn),
        in_specs=[
            pl.BlockSpec((tm, d), lambda i, j: (i, 0)),
            pl.BlockSpec((None, d, tn), lambda i, j: (layer, 0, j)),
            pl.BlockSpec((None, d, LANES), lambda i, j: (layer, 0, 0)),
        ],
        out_specs=[
            pl.BlockSpec((tm, tn), lambda i, j: (i, j)),
            pl.BlockSpec((tm, LANES), lambda i, j: (i, 0)),
        ],
        out_shape=[jax.ShapeDtypeStruct((t, n), BF16), jax.ShapeDtypeStruct((t, LANES), F32)],
        compiler_params=_cparams(("parallel", "arbitrary")),
        name="inproj",
    )(hb, w_mix, w_small)


def _gate_body(hb_ref, w_ref, b_ref, o_ref):
    z = _dot(hb_ref[...], w_ref[...]) + b_ref[...]
    o_ref[...] = _sigmoid(z).astype(BF16)


def _gateproj(hb, w_gate, b_gate, layer, tm=1024, tn=2048):
    t, d = hb.shape
    n = w_gate.shape[2]
    tm = min(tm, t)
    return pl.pallas_call(
        _gate_body,
        grid=(t // tm, n // tn),
        in_specs=[
            pl.BlockSpec((tm, d), lambda i, j: (i, 0)),
            pl.BlockSpec((None, d, tn), lambda i, j: (layer, 0, j)),
            pl.BlockSpec((None, 1, tn), lambda i, j: (layer, 0, j)),
        ],
        out_specs=pl.BlockSpec((tm, tn), lambda i, j: (i, j)),
        out_shape=jax.ShapeDtypeStruct((t, n), BF16),
        compiler_params=_cparams(("parallel", "arbitrary")),
        name="gateproj",
    )(hb, w_gate, b_gate)


def _rope(x, c, sa, sb, half):
    return x * c + pltpu.roll(x, LANES - half, 1) * sa + pltpu.roll(x, half, 1) * sb


def _causal_block(q_rows, ks_ref, vs_ref, blk, tq, n_maps):
    length = (blk + 1) * tq
    s = _dot_nt(q_rows, ks_ref[0:length, :])
    rows = lax.broadcasted_iota(jnp.int32, s.shape, 0)
    if n_maps > 1:
        rows = jnp.where(rows >= tq, rows - tq, rows)
    cols = lax.broadcasted_iota(jnp.int32, s.shape, 1)
    s = jnp.where(cols <= rows + blk * tq, s, NEG_BIG)
    m = jnp.max(s, axis=-1, keepdims=True)
    p = jnp.exp(s - m)
    l = jnp.sum(p, axis=-1, keepdims=True)
    o = _dot(p.astype(BF16), vs_ref[0:length, :])
    return o / l


def _diff_body(lam_init, tq, q_ref, k_ref, v_ref, c_ref, sa_ref, sb_ref, lam_ref, g_ref, o_ref, qs_ref, ks_ref):
    s_len = q_ref.shape[0]
    c, sa, sb = c_ref[...], sa_ref[...], sb_ref[...]
    q = _rope(q_ref[...].astype(F32), c, sa, sb, DIFF_ROT // 2) * (DIFF_HEAD_DIM ** -0.5)
    k = _rope(k_ref[...].astype(F32), c, sa, sb, DIFF_ROT // 2)
    first = lax.broadcasted_iota(jnp.int32, (1, LANES), 1) < DIFF_HEAD_DIM
    qs_ref[0] = jnp.where(first, q, 0.0).astype(BF16)
    qs_ref[1] = jnp.where(first, 0.0, q).astype(BF16)
    ks_ref[...] = k.astype(BF16)

    lp = lam_ref[...]
    lam = (jnp.exp(jnp.sum(lp[0:1] * lp[1:2], axis=-1, keepdims=True))
           - jnp.exp(jnp.sum(lp[2:3] * lp[3:4], axis=-1, keepdims=True)) + lam_init)
    g = g_ref[...]
    for blk in range(s_len // tq):
        r0 = blk * tq
        q_rows = jnp.concatenate([qs_ref[0, r0:r0 + tq, :], qs_ref[1, r0:r0 + tq, :]], axis=0)
        o = _causal_block(q_rows, ks_ref, v_ref, blk, tq, 2)
        od = o[:tq] - lam * o[tq:]
        o_ref[r0:r0 + tq, :] = (_rms(od, g) * (1.0 - lam_init)).astype(BF16)


def _diff_attention(p_mix, rope_d, lam, subln_g, layer, bsz, s_len, tq=512):
    t = p_mix.shape[0]
    lam_init = 0.8 - 0.6 * math.exp(-0.3 * layer)
    tq = min(tq, s_len)
    blk = lambda off: pl.BlockSpec((s_len, LANES), lambda b, h: (b, off + h))
    tab = pl.BlockSpec((None, s_len, LANES), lambda b, h: (b, 0, 0))
    return pl.pallas_call(
        functools.partial(_diff_body, lam_init, tq),
        grid=(bsz, DIFF_HEADS),
        in_specs=[blk(MIX_AQ), blk(MIX_AK), blk(MIX_AV), tab, tab, tab,
                  pl.BlockSpec((None, 4, DIFF_HEAD_DIM), lambda b, h: (layer, 0, 0)),
                  pl.BlockSpec((None, 1, LANES), lambda b, h: (layer, 0, 0))],
        out_specs=pl.BlockSpec((s_len, LANES), lambda b, h: (b, h)),
        out_shape=jax.ShapeDtypeStruct((t, BRANCH_WIDTH), BF16),
        scratch_shapes=[pltpu.VMEM((2, s_len, LANES), BF16), pltpu.VMEM((s_len, LANES), BF16)],
        compiler_params=_cparams(("parallel", "arbitrary")),
        name="diff_attn",
    )(p_mix, p_mix, p_mix, *rope_d, lam, subln_g)


def _mla_body(tq, cq_ref, ckv_ref, kr_ref, c_ref, sa_ref, sb_ref, qg_ref, kvg_ref, wq_ref, wkv_ref,
              o_ref, qs_ref, ks_ref, vs_ref):
    s_len = cq_ref.shape[0]
    c, sa, sb = c_ref[...], sa_ref[...], sb_ref[...]
    scale = (MLA_NOPE + MLA_ROPE) ** -0.5
    cqn = _rms(cq_ref[...].astype(F32), qg_ref[...]).astype(BF16)
    qc = _dot(cqn, wq_ref[...])
    qs_ref[:, 0:LANES] = (qc[:, 0:LANES] * scale).astype(BF16)
    qs_ref[:, LANES:] = (_rope(qc[:, LANES:], c, sa, sb, MLA_ROPE // 2) * scale).astype(BF16)
    ckvn = _rms(ckv_ref[...].astype(F32), kvg_ref[...]).astype(BF16)
    kv = _dot(ckvn, wkv_ref[...])
    ks_ref[:, 0:LANES] = kv[:, 0:LANES].astype(BF16)
    ks_ref[:, LANES:] = _rope(kr_ref[...].astype(F32), c, sa, sb, MLA_ROPE // 2).astype(BF16)
    vs_ref[...] = kv[:, LANES:].astype(BF16)
    for blk in range(s_len // tq):
        r0 = blk * tq
        o = _causal_block(qs_ref[r0:r0 + tq, :], ks_ref, vs_ref, blk, tq, 1)
        o_ref[r0:r0 + tq, :] = o.astype(BF16)


def _mla_attention(p_mix, rope_m, q_norm_g, kv_norm_g, w_uq, w_ukv, layer, bsz, s_len, tq=512):
    t = p_mix.shape[0]
    tq = min(tq, s_len)
    tab = pl.BlockSpec((None, s_len, LANES), lambda b, h: (b, 0, 0))
    return pl.pallas_call(
        functools.partial(_mla_body, tq),
        grid=(bsz, MLA_HEADS),
        in_specs=[pl.BlockSpec((s_len, MLA_Q_RANK), lambda b, h: (b, MIX_CQ)),
                  pl.BlockSpec((s_len, LANES), lambda b, h: (b, MIX_CKV)),
                  pl.BlockSpec((s_len, LANES), lambda b, h: (b, MIX_CKR)),
                  tab, tab, tab,
                  pl.BlockSpec((None, 1, MLA_Q_RANK), lambda b, h: (layer, 0, 0)),
                  pl.BlockSpec((None, 1, MLA_KV_RANK), lambda b, h: (layer, 0, 0)),
                  pl.BlockSpec((None, None, MLA_Q_RANK, 2 * LANES), lambda b, h: (layer, h, 0, 0)),
                  pl.BlockSpec((None, None, MLA_KV_RANK, 2 * LANES), lambda b, h: (layer, h, 0, 0))],
        out_specs=pl.BlockSpec((s_len, LANES), lambda b, h: (b, h)),
        out_shape=jax.ShapeDtypeStruct((t, BRANCH_WIDTH), BF16),
        scratch_shapes=[pltpu.VMEM((s_len, 2 * LANES), BF16), pltpu.VMEM((s_len, 2 * LANES), BF16),
                        pltpu.VMEM((s_len, LANES), BF16)],
        compiler_params=_cparams(("parallel", "arbitrary")),
        name="mla_attn",
    )(p_mix, p_mix, p_mix, *rope_m, q_norm_g, kv_norm_g, w_uq, w_ukv)


def _chunk_cumsum(x, chunk):
    pos = lax.broadcasted_iota(jnp.int32, x.shape, 0) % chunk
    step = 1
    while step < chunk:
        x = x + jnp.where(pos >= step, pltpu.roll(x, step, 0), 0.0)
        step *= 2
    return x


def _gla_body(q_ref, k_ref, v_ref, r_ref, ps_ref, wg_ref, bg_ref, ng_ref, o_ref, b_ref, st_ref):
    s_len = q_ref.shape[0]
    ck = GLA_CHUNK
    x = _dot(ps_ref[...].astype(BF16), wg_ref[...]) + bg_ref[...]
    g = (jnp.minimum(x, 0.0) - jnp.log(1.0 + jnp.exp(-jnp.abs(x)))) / GLA_TAU
    b_ref[...] = _chunk_cumsum(g, ck)
    st_ref[...] = jnp.zeros_like(st_ref)
    tril = (lax.broadcasted_iota(jnp.int32, (ck, ck), 0) >= lax.broadcasted_iota(jnp.int32, (ck, ck), 1))
    ng = ng_ref[...]

    def chunk(ci, carry):
        sl = pl.ds(pl.multiple_of(ci * ck, ck), ck)
        bc = b_ref[sl, :]
        bl = b_ref[pl.ds(ci * ck + ck - 1, 1), :]
        kc = k_ref[sl, :].astype(F32)
        qt = (q_ref[sl, :].astype(F32) * (GLA_DK ** -0.5) * jnp.exp(bc)).astype(BF16)
        kt = (kc * jnp.exp(-bc)).astype(BF16)
        kd = (kc * jnp.exp(bl - bc)).astype(BF16)
        vc = v_ref[sl, :]
        att = jnp.where(tril, _dot_nt(qt, kt), 0.0).astype(BF16)
        st = st_ref[...]
        o = _dot(att, vc) + _dot_nt(qt, st.astype(BF16))
        st_ref[...] = jnp.exp(bl) * st + _dot_tn(vc, kd)
        o_ref[sl, :] = (_rms(o, ng) * _silu(r_ref[sl, :].astype(F32))).astype(BF16)
        return carry

    lax.fori_loop(0, s_len // ck, chunk, 0)


def _gla(p_mix, p_small, w_gate2, b_gate, norm_g, layer, bsz, s_len):
    t = p_mix.shape[0]
    blk = lambda off: pl.BlockSpec((s_len, LANES), lambda b, h: (b, off + h))
    return pl.pallas_call(
        _gla_body,
        grid=(bsz, GLA_HEADS),
        in_specs=[blk(MIX_BQ), blk(MIX_BK), blk(MIX_BV), blk(MIX_BR),
                  pl.BlockSpec((s_len, LANES), lambda b, h: (b, 0)),
                  pl.BlockSpec((None, None, LANES, LANES), lambda b, h: (layer, h, 0, 0)),
                  pl.BlockSpec((None, None, 1, LANES), lambda b, h: (layer, h, 0, 0)),
                  pl.BlockSpec((None, 1, LANES), lambda b, h: (layer, 0, 0))],
        out_specs=pl.BlockSpec((s_len, LANES), lambda b, h: (b, h)),
        out_shape=jax.ShapeDtypeStruct((t, BRANCH_WIDTH), BF16),
        scratch_shapes=[pltpu.VMEM((s_len, LANES), F32), pltpu.VMEM((GLA_DV, LANES), F32)],
        compiler_params=_cparams(("parallel", "arbitrary")),
        name="gla",
    )(p_mix, p_mix, p_mix, p_mix, p_small, w_gate2, b_gate, norm_g)


def _causal_conv_silu(x, w, b):
    row = lax.broadcasted_iota(jnp.int32, x.shape, 0)
    y = x * w[SSD_CONV - 1:SSD_CONV] + b
    for shift in range(1, SSD_CONV):
        xs = jnp.where(row >= shift, pltpu.roll(x, shift, 0), 0.0)
        y = y + xs * w[SSD_CONV - 1 - shift:SSD_CONV - shift]
    return _silu(y)


def _lane_col(a, idx, rows):
    return jnp.broadcast_to(a[:, idx:idx + 1], (rows, LANES))


def _ssd_body(bc_ref, x_ref, z_ref, ps_ref, cw_bc_ref, cw_x_ref, cb_bc_ref, cb_x_ref, dtb_ref, alog_ref,
              dsk_ref, ng_ref, o_ref, xs_ref, bs_ref, cs_ref, dt_ref, ac_ref, st_ref):
    s_len = x_ref.shape[0]
    ck = SSD_CHUNK
    n_pairs = SSD_HEADS // 2
    xs_ref[...] = _causal_conv_silu(x_ref[...].astype(F32), cw_x_ref[...], cb_x_ref[...])
    bcv = _causal_conv_silu(bc_ref[...].astype(F32), cw_bc_ref[...], cb_bc_ref[...])
    bs_ref[...] = bcv[:, 0:SSD_GROUPS * SSD_STATE].astype(BF16)
    cs_ref[...] = bcv[:, SSD_GROUPS * SSD_STATE:].astype(BF16)
    dt = _softplus(ps_ref[...] + dtb_ref[...])
    dt_ref[...] = dt
    ac_ref[...] = _chunk_cumsum(dt * (-jnp.exp(alog_ref[...])), ck)
    st_ref[...] = jnp.zeros_like(st_ref)

    tril = (lax.broadcasted_iota(jnp.int32, (ck, ck), 0) >= lax.broadcasted_iota(jnp.int32, (ck, ck), 1))
    first = lax.broadcasted_iota(jnp.int32, (1, LANES), 1) < SSD_HEADDIM

    def chunk(ci, carry):
        sl = pl.ds(pl.multiple_of(ci * ck, ck), ck)
        a = ac_ref[sl, :]
        a_t = a.T
        a_last = ac_ref[pl.ds(ci * ck + ck - 1, 1), :]
        dtc = dt_ref[sl, :]
        for grp in range(SSD_GROUPS):
            bm = bs_ref[sl, grp * SSD_STATE:(grp + 1) * SSD_STATE]
            cm = cs_ref[sl, grp * SSD_STATE:(grp + 1) * SSD_STATE]
            cb = _dot_nt(cm, bm)
            ys = []
            for pp in range(n_pairs // SSD_GROUPS):
                pair = grp * (n_pairs // SSD_GROUPS) + pp
                h0 = SMALL_DT + 2 * pair
                a0, a1 = _lane_col(a, h0, ck), _lane_col(a, h0 + 1, ck)
                a_pair = jnp.where(first, a0, a1)
                dt_pair = jnp.where(first, _lane_col(dtc, h0, ck), _lane_col(dtc, h0 + 1, ck))
                al_pair = jnp.where(first, _lane_col(a_last, h0, 1), _lane_col(a_last, h0 + 1, 1))
                xp = xs_ref[sl, pair * LANES:(pair + 1) * LANES]
                xdt = xp * dt_pair
                m0 = (cb * jnp.where(tril, jnp.exp(a0 - a_t[h0:h0 + 1, :]), 0.0)).astype(BF16)
                m1 = (cb * jnp.where(tril, jnp.exp(a1 - a_t[h0 + 1:h0 + 2, :]), 0.0)).astype(BF16)
                y = (_dot(m0, jnp.where(first, xdt, 0.0).astype(BF16))
                     + _dot(m1, jnp.where(first, 0.0, xdt).astype(BF16)))
                st = st_ref[pair]
                y = y + _dot(cm, st.astype(BF16)) * jnp.exp(a_pair)
                st_ref[pair] = jnp.exp(al_pair) * st + _dot_tn(bm, (xdt * jnp.exp(al_pair - a_pair)).astype(BF16))
                ys.append(y + dsk_ref[:, pair * LANES:(pair + 1) * LANES] * xp)
            width = SSD_INNER // SSD_GROUPS
            lo = grp * width
            yg = jnp.concatenate(ys, axis=1) * _silu(z_ref[sl, lo:lo + width].astype(F32))
            o_ref[sl, lo:lo + width] = _rms(yg, ng_ref[:, lo:lo + width]).astype(BF16)
        return carry

    lax.fori_loop(0, s_len // ck, chunk, 0)


def _ssd(p_mix, p_small, conv_w, conv_b, dt_bias, a_log, d_skip, norm_g, layer, bsz, s_len):
    t = p_mix.shape[0]
    wide = SSD_INNER
    row = lambda width, idx: pl.BlockSpec((s_len, width), lambda b: (b, idx))
    par = lambda rows, width, idx: pl.BlockSpec((None, rows, width), lambda b: (layer, 0, idx))
    return pl.pallas_call(
        _ssd_body,
        grid=(bsz,),
        in_specs=[row(wide, MIX_SBC * LANES // wide), row(wide, MIX_SX * LANES // wide),
                  row(wide, MIX_SZ * LANES // wide), row(LANES, 0),
                  par(SSD_CONV, wide, 1), par(SSD_CONV, wide, 0), par(1, wide, 1), par(1, wide, 0),
                  par(1, LANES, 0), par(1, LANES, 0), par(1, wide, 0), par(1, wide, 0)],
        out_specs=pl.BlockSpec((s_len, wide), lambda b: (b, 0)),
        out_shape=jax.ShapeDtypeStruct((t, wide), BF16),
        scratch_shapes=[pltpu.VMEM((s_len, wide), F32), pltpu.VMEM((s_len, SSD_GROUPS * SSD_STATE), BF16),
                        pltpu.VMEM((s_len, SSD_GROUPS * SSD_STATE), BF16), pltpu.VMEM((s_len, LANES), F32),
                        pltpu.VMEM((s_len, LANES), F32), pltpu.VMEM((SSD_HEADS // 2, SSD_STATE, LANES), F32)],
        compiler_params=_cparams(("parallel",)),
        name="ssd",
    )(p_mix, p_mix, p_mix, p_small, conv_w, conv_w, conv_b, conv_b, dt_bias, a_log, d_skip, norm_g)


def _merge_body(alpha, oa_ref, ob_ref, oc_ref, od_ref, gt_ref, h_ref, wb_ref, wo_ref, g_ref, b_ref, o_ref, obf_ref):
    d = h_ref.shape[1]
    merged = None
    for i, br in enumerate((oa_ref, ob_ref, oc_ref, od_ref)):
        y = gt_ref[:, i * d:(i + 1) * d].astype(F32) * _dot(br[...], wb_ref[i])
        merged = y if merged is None else merged + y
    y = alpha * h_ref[...] + _dot(merged.astype(BF16), wo_ref[...])
    out = _layer_norm(y, g_ref[...], b_ref[...])
    o_ref[...] = out
    obf_ref[...] = out.astype(BF16)


def _merge(outs, gates, h, w_branch, w_out, ln_g, ln_b, layer, alpha, tm=256):
    t, d = h.shape
    tm = min(tm, t)
    bw = outs[0].shape[1]
    once = pl.Buffered(1)
    return pl.pallas_call(
        functools.partial(_merge_body, alpha),
        grid=(t // tm,),
        in_specs=[pl.BlockSpec((tm, bw), lambda i: (i, 0))] * 4 + [
            pl.BlockSpec((tm, N_BRANCH * d), lambda i: (i, 0)),
            pl.BlockSpec((tm, d), lambda i: (i, 0)),
            pl.BlockSpec((None, N_BRANCH, bw, d), lambda i: (layer, 0, 0, 0), pipeline_mode=once),
            pl.BlockSpec((None, d, d), lambda i: (layer, 0, 0), pipeline_mode=once),
            pl.BlockSpec((None, None, 1, d), lambda i: (layer, 1, 0, 0)),
            pl.BlockSpec((None, None, 1, d), lambda i: (layer, 1, 0, 0)),
        ],
        out_specs=[pl.BlockSpec((tm, d), lambda i: (i, 0)), pl.BlockSpec((tm, d), lambda i: (i, 0))],
        out_shape=[jax.ShapeDtypeStruct((t, d), F32), jax.ShapeDtypeStruct((t, d), BF16)],
        compiler_params=_cparams(("parallel",)),
        name="merge",
    )(*outs, gates, h, w_branch, w_out, ln_g, ln_b)


def _rope_lane_tables(positions, rot_dim, group):
    inv = ROPE_THETA ** (-jnp.arange(0, rot_dim, 2, dtype=F32) / rot_dim)
    ang = positions.astype(F32)[..., None] * inv
    cos, sin = jnp.cos(ang), jnp.sin(ang)
    half = rot_dim // 2
    rest = group - rot_dim
    shape = cos.shape[:-1]
    ones = jnp.ones(shape + (rest,), F32)
    c = jnp.concatenate([cos, cos, ones], axis=-1)
    sa = jnp.concatenate([-sin, jnp.zeros(shape + (half + rest,), F32)], axis=-1)
    sb = jnp.concatenate([jnp.zeros(shape + (half,), F32), sin, jnp.zeros(shape + (rest,), F32)], axis=-1)
    reps = LANES // group
    return tuple(jnp.tile(t, (1, 1, reps)) for t in (c, sa, sb))


def _pad_last(x, width):
    return jnp.pad(x, [(0, 0)] * (x.ndim - 1) + [(0, width - x.shape[-1])])


def _prepare_in_proj(w_in):
    offs = [0]
    for sz in IN_SPLITS:
        offs.append(offs[-1] + sz)
    seg = lambda i: w_in[..., offs[i]:offs[i + 1]]
    (a_q, a_k, a_v, b_q, b_k, b_v, b_glow, b_r, c_q, c_kv, c_kr, d_z, d_xbc, d_dt, gate) = [seg(i) for i in range(15)]
    lead = w_in.shape[:-1]

    def pad_heads(w):
        return _pad_last(w.reshape(lead + (GLA_HEADS, GLA_DK)), LANES).reshape(lead + (GLA_HEADS * LANES,))

    d_x, d_bc = d_xbc[..., :SSD_INNER], d_xbc[..., SSD_INNER:]
    w_mix = jnp.concatenate([c_q, c_kv, a_q, a_k, a_v, pad_heads(b_q), pad_heads(b_k), b_v, b_r,
                             d_bc, d_x, d_z, _pad_last(c_kr, LANES)], axis=-1).astype(BF16)
    w_small = _pad_last(jnp.concatenate([b_glow, d_dt], axis=-1), LANES).astype(BF16)
    return w_mix, w_small, gate.astype(BF16)


def kernel(x, positions, ln_g, ln_b, ffn1_w_gu, ffn1_w_down, ffn2_w_gu, ffn2_w_down, w_in, b_gate, diff_lambda, diff_subln_g, gla_w_gate2, gla_b_gate, gla_norm_g, mla_q_norm_g, mla_w_uq, mla_kv_norm_g, mla_w_ukv, ssd_conv_w, ssd_conv_b, ssd_dt_bias, ssd_a_log, ssd_d, ssd_norm_g, w_branch, w_out):
    bsz, s_len, d = x.shape
    depth = ln_g.shape[0]
    alpha = (2 * depth) ** 0.25
    t = bsz * s_len

    rope_d = _rope_lane_tables(positions, DIFF_ROT, DIFF_HEAD_DIM)
    rope_m = _rope_lane_tables(positions, MLA_ROPE, LANES)

    w1_gu, w1_dn = ffn1_w_gu.astype(BF16), ffn1_w_down.astype(BF16)
    w2_gu, w2_dn = ffn2_w_gu.astype(BF16), ffn2_w_down.astype(BF16)
    w_mix, w_small, w_gate = _prepare_in_proj(w_in)
    b_gate3 = b_gate[:, None, :]
    ln_g4, ln_b4 = ln_g[:, :, None, :], ln_b[:, :, None, :]
    subln3 = diff_subln_g[:, None, :]
    wg2 = gla_w_gate2.reshape(depth, GLA_GATE_RANK, GLA_HEADS, GLA_DK).transpose(0, 2, 1, 3)
    wg2 = jnp.pad(wg2, ((0, 0), (0, 0), (SMALL_GLOW, LANES - GLA_GATE_RANK - SMALL_GLOW), (0, LANES - GLA_DK))).astype(BF16)
    bg2 = _pad_last(gla_b_gate.reshape(depth, GLA_HEADS, 1, GLA_DK), LANES)
    gla_ng3 = gla_norm_g[:, None, :]
    wq = mla_w_uq.reshape(depth, MLA_Q_RANK, MLA_HEADS, MLA_NOPE + MLA_ROPE).transpose(0, 2, 1, 3)
    wq = _pad_last(wq, 2 * LANES).astype(BF16)
    wkv = mla_w_ukv.reshape(depth, MLA_KV_RANK, MLA_HEADS, MLA_NOPE + MLA_V).transpose(0, 2, 1, 3).astype(BF16)
    mla_qg3, mla_kvg3 = mla_q_norm_g[:, None, :], mla_kv_norm_g[:, None, :]
    dtb3 = jnp.pad(ssd_dt_bias, ((0, 0), (SMALL_DT, LANES - SMALL_DT - SSD_HEADS)))[:, None, :]
    alog3 = jnp.pad(ssd_a_log, ((0, 0), (SMALL_DT, LANES - SMALL_DT - SSD_HEADS)))[:, None, :]
    dsk3 = jnp.repeat(ssd_d, SSD_HEADDIM, axis=-1)[:, None, :]
    cb3 = ssd_conv_b[:, None, :]
    ssd_ng3 = ssd_norm_g[:, None, :]
    wbr = w_branch.astype(BF16)
    wout = w_out.astype(BF16)

    h = x.reshape(t, d)
    for l in range(depth):
        h, hb = _ffn(h, w1_gu, w1_dn, ln_g4, ln_b4, l, 0, alpha)
        p_mix, p_small = _inproj(hb, w_mix, w_small, l)
        gates = _gateproj(hb, w_gate, b_gate3, l)
        o_a = _diff_attention(p_mix, rope_d, diff_lambda, subln3, l, bsz, s_len)
        o_b = _gla(p_mix, p_small, wg2, bg2, gla_ng3, l, bsz, s_len)
        o_c = _mla_attention(p_mix, rope_m, mla_qg3, mla_kvg3, wq, wkv, l, bsz, s_len)
        o_d = _ssd(p_mix, p_small, ssd_conv_w, cb3, dtb3, alog3, dsk3, ssd_ng3, l, bsz, s_len)
        h, hb = _merge((o_a, o_b, o_c, o_d), gates, h, wbr, wout, ln_g4, ln_b4, l, alpha)
        h, hb = _ffn(h, w2_gu, w2_dn, ln_g4, ln_b4, l, 2, alpha)
    return h.reshape(bsz, s_len, d)
```

```python
import functools
import math

import jax
import jax.numpy as jnp
from jax import lax
from jax.experimental import pallas as pl
from jax.experimental.pallas import tpu as pltpu

F32 = jnp.float32
BF16 = jnp.bfloat16

D_MODEL = 2048
N_BRANCH = 4
BRANCH_WIDTH = D_MODEL // 4
ROPE_THETA = 500000.0
NORM_EPS = 1e-5

DIFF_HEADS = 4
DIFF_HEAD_DIM = 64
DIFF_ROT = 16

GLA_HEADS = 4
GLA_DK = 64
GLA_DV = 128
GLA_GATE_RANK = 16
GLA_TAU = 16.0
GLA_CHUNK = 64
GLA_UNROLL = 8

MLA_HEADS = 4
MLA_NOPE = 128
MLA_ROPE = 64
MLA_V = 128
MLA_Q_RANK = 384
MLA_KV_RANK = 128

SSD_HEADDIM = 64
SSD_INNER = 512
SSD_HEADS = 8
SSD_GROUPS = 2
SSD_STATE = 128
SSD_CONV = 4
SSD_CHUNK = 128

D_FF = 5632

IN_SPLITS = [512, 512, 512, 256, 256, 512, 16, 512, 384, 128, 64, 512, 1024, 8, 4 * D_MODEL]

LANES = 128
VMEM_LIMIT_BYTES = 56 * 2**20

MIX_CQ = 0
MIX_CKV = 3
MIX_AQ = 4
MIX_AK = 8
MIX_AV = 12
MIX_BQ = 16
MIX_BK = 20
MIX_BV = 24
MIX_BR = 28
MIX_SBC = 32
MIX_SX = 36
MIX_SZ = 40
MIX_CKR = 44
MIX_WIDTH = 45 * LANES
SMALL_GLOW = 0
SMALL_DT = 16

NEG_BIG = -1e30


def _cparams(semantics):
    return pltpu.CompilerParams(dimension_semantics=semantics, vmem_limit_bytes=VMEM_LIMIT_BYTES)


def _layer_norm(y, g, b):
    mu = jnp.mean(y, axis=-1, keepdims=True)
    yc = y - mu
    var = jnp.mean(yc * yc, axis=-1, keepdims=True)
    return yc * lax.rsqrt(var + NORM_EPS) * g + b


def _rms(x, g):
    return x * lax.rsqrt(jnp.mean(x * x, axis=-1, keepdims=True) + NORM_EPS) * g


def _sigmoid(x):
    return 1.0 / (1.0 + jnp.exp(-x))


def _silu(x):
    return x * _sigmoid(x)


def _softplus(x):
    return jnp.maximum(x, 0.0) + jnp.log(1.0 + jnp.exp(-jnp.abs(x)))


def _dot(a, b):
    return jnp.dot(a, b, preferred_element_type=F32)


def _dot_nt(a, b):
    return lax.dot_general(a, b, (((1,), (1,)), ((), ())), preferred_element_type=F32)


def _dot_tn(a, b):
    return lax.dot_general(a, b, (((0,), (0,)), ((), ())), preferred_element_type=F32)


def _ffn_body(alpha, nj, h_ref, wg_ref, wu_ref, wd_ref, g_ref, b_ref, o_ref, ob_ref, hb_ref, acc_ref):
    j = pl.program_id(1)

    @pl.when(j == 0)
    def _():
        h = h_ref[...]
        hb_ref[...] = h.astype(BF16)
        acc_ref[...] = (2.0 * alpha) * h

    hb = hb_ref[...]
    gate = _dot(hb, wg_ref[...])
    up = _dot(hb, wu_ref[...])
    act = (_silu(gate) * up).astype(BF16)
    acc_ref[...] += _dot(act, wd_ref[...])

    @pl.when(j == nj - 1)
    def _():
        out = _layer_norm(0.5 * acc_ref[...], g_ref[...], b_ref[...])
        o_ref[...] = out
        ob_ref[...] = out.astype(BF16)


def _ffn(h, w_gu, w_down, ln_g, ln_b, layer, ln_idx, alpha, tm=512, tf=512):
    t, d = h.shape
    dff = w_down.shape[1]
    nj = dff // tf
    grid = (t // tm, nj)
    return pl.pallas_call(
        functools.partial(_ffn_body, alpha, nj),
        grid=grid,
        in_specs=[
            pl.BlockSpec((tm, d), lambda i, j: (i, 0)),
            pl.BlockSpec((None, d, tf), lambda i, j: (layer, 0, j)),
            pl.BlockSpec((None, d, tf), lambda i, j: (layer, 0, nj + j)),
            pl.BlockSpec((None, tf, d), lambda i, j: (layer, j, 0)),
            pl.BlockSpec((None, None, 1, d), lambda i, j: (layer, ln_idx, 0, 0)),
            pl.BlockSpec((None, None, 1, d), lambda i, j: (layer, ln_idx, 0, 0)),
        ],
        out_specs=[
            pl.BlockSpec((tm, d), lambda i, j: (i, 0)),
            pl.BlockSpec((tm, d), lambda i, j: (i, 0)),
        ],
        out_shape=[jax.ShapeDtypeStruct((t, d), F32), jax.ShapeDtypeStruct((t, d), BF16)],
        scratch_shapes=[pltpu.VMEM((tm, d), BF16), pltpu.VMEM((tm, d), F32)],
        compiler_params=_cparams(("parallel", "arbitrary")),
        name="ffn",
    )(h, w_gu, w_gu, w_down, ln_g, ln_b)


def _inproj_body(hb_ref, w_ref, ws_ref, p_ref, ps_ref):
    hb = hb_ref[...]
    p_ref[...] = _dot(hb, w_ref[...]).astype(BF16)

    @pl.when(pl.program_id(1) == 0)
    def _():
        ps_ref[...] = _dot(hb, ws_ref[...])


def _inproj(hb, w_mix, w_small, layer, tm=1024, tn=1920):
    t, d = hb.shape
    n = w_mix.shape[2]
    tm = min(tm, t)
    return pl.pallas_call(
        _inproj_body,
        grid=(t // tm, n // tn),
        in_specs=[
            pl.BlockSpec((tm, d), lambda i, j: (i, 0)),
            pl.BlockSpec((None, d, tn), lambda i, j: (layer, 0, j)),
            pl.BlockSpec((None, d, LANES), lambda i, j: (layer, 0, 0)),
        ],
        out_specs=[
            pl.BlockSpec((tm, tn), lambda i, j: (i, j)),
            pl.BlockSpec((tm, LANES), lambda i, j: (i, 0)),
        ],
        out_shape=[jax.ShapeDtypeStruct((t, n), BF16), jax.ShapeDtypeStruct((t, LANES), F32)],
        compiler_params=_cparams(("parallel", "arbitrary")),
        name="inproj",
    )(hb, w_mix, w_small)


def _gate_body(hb_ref, w_ref, b_ref, o_ref):
    z = _dot(hb_ref[...], w_ref[...]) + b_ref[...]
    o_ref[...] = _sigmoid(z).astype(BF16)


def _gateproj(hb, w_gate, b_gate, layer, tm=1024, tn=2048):
    t, d = hb.shape
    n = w_gate.shape[2]
    tm = min(tm, t)
    return pl.pallas_call(
        _gate_body,
        grid=(t // tm, n // tn),
        in_specs=[
            pl.BlockSpec((tm, d), lambda i, j: (i, 0)),
            pl.BlockSpec((None, d, tn), lambda i, j: (layer, 0, j)),
            pl.BlockSpec((None, 1, tn), lambda i, j: (layer, 0, j)),
        ],
        out_specs=pl.BlockSpec((tm, tn), lambda i, j: (i, j)),
        out_shape=jax.ShapeDtypeStruct((t, n), BF16),
        compiler_params=_cparams(("parallel", "arbitrary")),
        name="gateproj",
    )(hb, w_gate, b_gate)


def _rope(x, c, sa, sb, half):
    return x * c + pltpu.roll(x, LANES - half, 1) * sa + pltpu.roll(x, half, 1) * sb


def _causal_block(q_rows, ks_ref, vs_ref, blk, tq, n_maps):
    length = (blk + 1) * tq
    s = _dot_nt(q_rows, ks_ref[0:length, :])
    rows = lax.broadcasted_iota(jnp.int32, s.shape, 0)
    if n_maps > 1:
        rows = jnp.where(rows >= tq, rows - tq, rows)
    cols = lax.broadcasted_iota(jnp.int32, s.shape, 1)
    s = jnp.where(cols <= rows + blk * tq, s, NEG_BIG)
    m = jnp.max(s, axis=-1, keepdims=True)
    p = jnp.exp(s - m)
    l = jnp.sum(p, axis=-1, keepdims=True)
    o = _dot(p.astype(BF16), vs_ref[0:length, :])
    return o / l


def _diff_body(lam_init, tq, q_ref, k_ref, v_ref, c_ref, sa_ref, sb_ref, lam_ref, g_ref, o_ref, qs_ref, ks_ref):
    s_len = q_ref.shape[0]
    c, sa, sb = c_ref[...], sa_ref[...], sb_ref[...]
    q = _rope(q_ref[...].astype(F32), c, sa, sb, DIFF_ROT // 2) * (DIFF_HEAD_DIM ** -0.5)
    k = _rope(k_ref[...].astype(F32), c, sa, sb, DIFF_ROT // 2)
    first = lax.broadcasted_iota(jnp.int32, (1, LANES), 1) < DIFF_HEAD_DIM
    qs_ref[0] = jnp.where(first, q, 0.0).astype(BF16)
    qs_ref[1] = jnp.where(first, 0.0, q).astype(BF16)
    ks_ref[...] = k.astype(BF16)

    lp = lam_ref[...]
    lam = (jnp.exp(jnp.sum(lp[0:1] * lp[1:2], axis=-1, keepdims=True))
           - jnp.exp(jnp.sum(lp[2:3] * lp[3:4], axis=-1, keepdims=True)) + lam_init)
    g = g_ref[...]
    for blk in range(s_len // tq):
        r0 = blk * tq
        q_rows = jnp.concatenate([qs_ref[0, r0:r0 + tq, :], qs_ref[1, r0:r0 + tq, :]], axis=0)
        o = _causal_block(q_rows, ks_ref, v_ref, blk, tq, 2)
        od = o[:tq] - lam * o[tq:]
        o_ref[r0:r0 + tq, :] = (_rms(od, g) * (1.0 - lam_init)).astype(BF16)


def _diff_attention(p_mix, rope_d, lam, subln_g, layer, bsz, s_len, tq=512):
    t = p_mix.shape[0]
    lam_init = 0.8 - 0.6 * math.exp(-0.3 * layer)
    tq = min(tq, s_len)
    blk = lambda off: pl.BlockSpec((s_len, LANES), lambda b, h: (b, off + h))
    tab = pl.BlockSpec((None, s_len, LANES), lambda b, h: (b, 0, 0))
    return pl.pallas_call(
        functools.partial(_diff_body, lam_init, tq),
        grid=(bsz, DIFF_HEADS),
        in_specs=[blk(MIX_AQ), blk(MIX_AK), blk(MIX_AV), tab, tab, tab,
                  pl.BlockSpec((None, 4, DIFF_HEAD_DIM), lambda b, h: (layer, 0, 0)),
                  pl.BlockSpec((None, 1, LANES), lambda b, h: (layer, 0, 0))],
        out_specs=pl.BlockSpec((s_len, LANES), lambda b, h: (b, h)),
        out_shape=jax.ShapeDtypeStruct((t, BRANCH_WIDTH), BF16),
        scratch_shapes=[pltpu.VMEM((2, s_len, LANES), BF16), pltpu.VMEM((s_len, LANES), BF16)],
        compiler_params=_cparams(("parallel", "arbitrary")),
        name="diff_attn",
    )(p_mix, p_mix, p_mix, *rope_d, lam, subln_g)


def _mla_body(tq, cq_ref, ckv_ref, kr_ref, c_ref, sa_ref, sb_ref, qg_ref, kvg_ref, wq_ref, wkv_ref,
              o_ref, qs_ref, ks_ref, vs_ref):
    s_len = cq_ref.shape[0]
    c, sa, sb = c_ref[...], sa_ref[...], sb_ref[...]
    scale = (MLA_NOPE + MLA_ROPE) ** -0.5
    cqn = _rms(cq_ref[...].astype(F32), qg_ref[...]).astype(BF16)
    qc = _dot(cqn, wq_ref[...])
    qs_ref[:, 0:LANES] = (qc[:, 0:LANES] * scale).astype(BF16)
    qs_ref[:, LANES:] = (_rope(qc[:, LANES:], c, sa, sb, MLA_ROPE // 2) * scale).astype(BF16)
    ckvn = _rms(ckv_ref[...].astype(F32), kvg_ref[...]).astype(BF16)
    kv = _dot(ckvn, wkv_ref[...])
    ks_ref[:, 0:LANES] = kv[:, 0:LANES].astype(BF16)
    ks_ref[:, LANES:] = _rope(kr_ref[...].astype(F32), c, sa, sb, MLA_ROPE // 2).astype(BF16)
    vs_ref[...] = kv[:, LANES:].astype(BF16)
    for blk in range(s_len // tq):
        r0 = blk * tq
        o = _causal_block(qs_ref[r0:r0 + tq, :], ks_ref, vs_ref, blk, tq, 1)
        o_ref[r0:r0 + tq, :] = o.astype(BF16)


def _mla_attention(p_mix, rope_m, q_norm_g, kv_norm_g, w_uq, w_ukv, layer, bsz, s_len, tq=512):
    t = p_mix.shape[0]
    tq = min(tq, s_len)
    tab = pl.BlockSpec((None, s_len, LANES), lambda b, h: (b, 0, 0))
    return pl.pallas_call(
        functools.partial(_mla_body, tq),
        grid=(bsz, MLA_HEADS),
        in_specs=[pl.BlockSpec((s_len, MLA_Q_RANK), lambda b, h: (b, MIX_CQ)),
                  pl.BlockSpec((s_len, LANES), lambda b, h: (b, MIX_CKV)),
                  pl.BlockSpec((s_len, LANES), lambda b, h: (b, MIX_CKR)),
                  tab, tab, tab,
                  pl.BlockSpec((None, 1, MLA_Q_RANK), lambda b, h: (layer, 0, 0)),
                  pl.BlockSpec((None, 1, MLA_KV_RANK), lambda b, h: (layer, 0, 0)),
                  pl.BlockSpec((None, None, MLA_Q_RANK, 2 * LANES), lambda b, h: (layer, h, 0, 0)),
                  pl.BlockSpec((None, None, MLA_KV_RANK, 2 * LANES), lambda b, h: (layer, h, 0, 0))],
        out_specs=pl.BlockSpec((s_len, LANES), lambda b, h: (b, h)),
        out_shape=jax.ShapeDtypeStruct((t, BRANCH_WIDTH), BF16),
        scratch_shapes=[pltpu.VMEM((s_len, 2 * LANES), BF16), pltpu.VMEM((s_len, 2 * LANES), BF16),
                        pltpu.VMEM((s_len, LANES), BF16)],
        compiler_params=_cparams(("parallel", "arbitrary")),
        name="mla_attn",
    )(p_mix, p_mix, p_mix, *rope_m, q_norm_g, kv_norm_g, w_uq, w_ukv)


def _chunk_cumsum(x, chunk):
    pos = lax.broadcasted_iota(jnp.int32, x.shape, 0) % chunk
    step = 1
    while step < chunk:
        x = x + jnp.where(pos >= step, pltpu.roll(x, step, 0), 0.0)
        step *= 2
    return x


def _gla_body(q_ref, k_ref, v_ref, r_ref, ps_ref, wg_ref, bg_ref, ng_ref, o_ref, b_ref):
    s_len = q_ref.shape[0]
    ck = GLA_CHUNK
    x = _dot(ps_ref[...].astype(BF16), wg_ref[...]) + bg_ref[...]
    g = (jnp.minimum(x, 0.0) - jnp.log(1.0 + jnp.exp(-jnp.abs(x)))) / GLA_TAU
    b_ref[...] = _chunk_cumsum(g, ck)
    tril = (lax.broadcasted_iota(jnp.int32, (ck, ck), 0) >= lax.broadcasted_iota(jnp.int32, (ck, ck), 1))
    ng = ng_ref[...]

    def chunk(ci, st):
        sl = pl.ds(pl.multiple_of(ci * ck, ck), ck)
        bc = b_ref[sl, :]
        bl = b_ref[pl.ds(ci * ck + ck - 1, 1), :]
        kc = k_ref[sl, :].astype(F32)
        qt = (q_ref[sl, :].astype(F32) * (GLA_DK ** -0.5) * jnp.exp(bc)).astype(BF16)
        kt = (kc * jnp.exp(-bc)).astype(BF16)
        kd = (kc * jnp.exp(bl - bc)).astype(BF16)
        vc = v_ref[sl, :]
        att = jnp.where(tril, _dot_nt(qt, kt), 0.0).astype(BF16)
        o = _dot(att, vc) + _dot_nt(qt, st.astype(BF16))
        o_ref[sl, :] = (_rms(o, ng) * _silu(r_ref[sl, :].astype(F32))).astype(BF16)
        return jnp.exp(bl) * st + _dot_tn(vc, kd)

    n_chunks = s_len // ck
    lax.fori_loop(0, n_chunks, chunk, jnp.zeros((GLA_DV, LANES), F32), unroll=math.gcd(n_chunks, GLA_UNROLL))


def _gla(p_mix, p_small, w_gate2, b_gate, norm_g, layer, bsz, s_len):
    t = p_mix.shape[0]
    blk = lambda off: pl.BlockSpec((s_len, LANES), lambda b, h: (b, off + h))
    return pl.pallas_call(
        _gla_body,
        grid=(bsz, GLA_HEADS),
        in_specs=[blk(MIX_BQ), blk(MIX_BK), blk(MIX_BV), blk(MIX_BR),
                  pl.BlockSpec((s_len, LANES), lambda b, h: (b, 0)),
                  pl.BlockSpec((None, None, LANES, LANES), lambda b, h: (layer, h, 0, 0)),
                  pl.BlockSpec((None, None, 1, LANES), lambda b, h: (layer, h, 0, 0)),
                  pl.BlockSpec((None, 1, LANES), lambda b, h: (layer, 0, 0))],
        out_specs=pl.BlockSpec((s_len, LANES), lambda b, h: (b, h)),
        out_shape=jax.ShapeDtypeStruct((t, BRANCH_WIDTH), BF16),
        scratch_shapes=[pltpu.VMEM((s_len, LANES), F32)],
        compiler_params=_cparams(("parallel", "arbitrary")),
        name="gla",
    )(p_mix, p_mix, p_mix, p_mix, p_small, w_gate2, b_gate, norm_g)


def _causal_conv_silu(x, w, b):
    row = lax.broadcasted_iota(jnp.int32, x.shape, 0)
    y = x * w[SSD_CONV - 1:SSD_CONV] + b
    for shift in range(1, SSD_CONV):
        xs = jnp.where(row >= shift, pltpu.roll(x, shift, 0), 0.0)
        y = y + xs * w[SSD_CONV - 1 - shift:SSD_CONV - shift]
    return _silu(y)


def _lane_col(a, idx, rows):
    return jnp.broadcast_to(a[:, idx:idx + 1], (rows, LANES))


def _ssd_body(bc_ref, x_ref, z_ref, ps_ref, cw_bc_ref, cw_x_ref, cb_bc_ref, cb_x_ref, dtb_ref, alog_ref,
              dsk_ref, ng_ref, o_ref, xs_ref, bs_ref, cs_ref, dt_ref, ac_ref, st_ref):
    s_len = x_ref.shape[0]
    ck = SSD_CHUNK
    n_pairs = SSD_HEADS // 2
    xs_ref[...] = _causal_conv_silu(x_ref[...].astype(F32), cw_x_ref[...], cb_x_ref[...])
    bcv = _causal_conv_silu(bc_ref[...].astype(F32), cw_bc_ref[...], cb_bc_ref[...])
    bs_ref[...] = bcv[:, 0:SSD_GROUPS * SSD_STATE].astype(BF16)
    cs_ref[...] = bcv[:, SSD_GROUPS * SSD_STATE:].astype(BF16)
    dt = _softplus(ps_ref[...] + dtb_ref[...])
    dt_ref[...] = dt
    ac_ref[...] = _chunk_cumsum(dt * (-jnp.exp(alog_ref[...])), ck)
    st_ref[...] = jnp.zeros_like(st_ref)

    tril = (lax.broadcasted_iota(jnp.int32, (ck, ck), 0) >= lax.broadcasted_iota(jnp.int32, (ck, ck), 1))
    first = lax.broadcasted_iota(jnp.int32, (1, LANES), 1) < SSD_HEADDIM

    def chunk(ci, carry):
        sl = pl.ds(pl.multiple_of(ci * ck, ck), ck)
        a = ac_ref[sl, :]
        a_t = a.T
        a_last = ac_ref[pl.ds(ci * ck + ck - 1, 1), :]
        dtc = dt_ref[sl, :]
        for grp in range(SSD_GROUPS):
            bm = bs_ref[sl, grp * SSD_STATE:(grp + 1) * SSD_STATE]
            cm = cs_ref[sl, grp * SSD_STATE:(grp + 1) * SSD_STATE]
            cb = _dot_nt(cm, bm)
            ys = []
            for pp in range(n_pairs // SSD_GROUPS):
                pair = grp * (n_pairs // SSD_GROUPS) + pp
                h0 = SMALL_DT + 2 * pair
                a0, a1 = _lane_col(a, h0, ck), _lane_col(a, h0 + 1, ck)
                a_pair = jnp.where(first, a0, a1)
                dt_pair = jnp.where(first, _lane_col(dtc, h0, ck), _lane_col(dtc, h0 + 1, ck))
                al_pair = jnp.where(first, _lane_col(a_last, h0, 1), _lane_col(a_last, h0 + 1, 1))
                xp = xs_ref[sl, pair * LANES:(pair + 1) * LANES]
                xdt = xp * dt_pair
                m0 = (cb * jnp.where(tril, jnp.exp(a0 - a_t[h0:h0 + 1, :]), 0.0)).astype(BF16)
                m1 = (cb * jnp.where(tril, jnp.exp(a1 - a_t[h0 + 1:h0 + 2, :]), 0.0)).astype(BF16)
                y = (_dot(m0, jnp.where(first, xdt, 0.0).astype(BF16))
                     + _dot(m1, jnp.where(first, 0.0, xdt).astype(BF16)))
                st = st_ref[pair]
                y = y + _dot(cm, st.astype(BF16)) * jnp.exp(a_pair)
                st_ref[pair] = jnp.exp(al_pair) * st + _dot_tn(bm, (xdt * jnp.exp(al_pair - a_pair)).astype(BF16))
                ys.append(y + dsk_ref[:, pair * LANES:(pair + 1) * LANES] * xp)
            width = SSD_INNER // SSD_GROUPS
            lo = grp * width
            yg = jnp.concatenate(ys, axis=1) * _silu(z_ref[sl, lo:lo + width].astype(F32))
            o_ref[sl, lo:lo + width] = _rms(yg, ng_ref[:, lo:lo + width]).astype(BF16)
        return carry

    n_chunks = s_len // ck
    lax.fori_loop(0, n_chunks, chunk, 0, unroll=math.gcd(n_chunks, 2))


def _ssd(p_mix, p_small, conv_w, conv_b, dt_bias, a_log, d_skip, norm_g, layer, bsz, s_len):
    t = p_mix.shape[0]
    wide = SSD_INNER
    row = lambda width, idx: pl.BlockSpec((s_len, width), lambda b: (b, idx))
    par = lambda rows, width, idx: pl.BlockSpec((None, rows, width), lambda b: (layer, 0, idx))
    return pl.pallas_call(
        _ssd_body,
        grid=(bsz,),
        in_specs=[row(wide, MIX_SBC * LANES // wide), row(wide, MIX_SX * LANES // wide),
                  row(wide, MIX_SZ * LANES // wide), row(LANES, 0),
                  par(SSD_CONV, wide, 1), par(SSD_CONV, wide, 0), par(1, wide, 1), par(1, wide, 0),
                  par(1, LANES, 0), par(1, LANES, 0), par(1, wide, 0), par(1, wide, 0)],
        out_specs=pl.BlockSpec((s_len, wide), lambda b: (b, 0)),
        out_shape=jax.ShapeDtypeStruct((t, wide), BF16),
        scratch_shapes=[pltpu.VMEM((s_len, wide), F32), pltpu.VMEM((s_len, SSD_GROUPS * SSD_STATE), BF16),
                        pltpu.VMEM((s_len, SSD_GROUPS * SSD_STATE), BF16), pltpu.VMEM((s_len, LANES), F32),
                        pltpu.VMEM((s_len, LANES), F32), pltpu.VMEM((SSD_HEADS // 2, SSD_STATE, LANES), F32)],
        compiler_params=_cparams(("parallel",)),
        name="ssd",
    )(p_mix, p_mix, p_mix, p_small, conv_w, conv_w, conv_b, conv_b, dt_bias, a_log, d_skip, norm_g)


def _merge_body(alpha, oa_ref, ob_ref, oc_ref, od_ref, gt_ref, h_ref, wb_ref, wo_ref, g_ref, b_ref, o_ref, obf_ref):
    d = h_ref.shape[1]
    merged = None
    for i, br in enumerate((oa_ref, ob_ref, oc_ref, od_ref)):
        y = gt_ref[:, i * d:(i + 1) * d].astype(F32) * _dot(br[...], wb_ref[i])
        merged = y if merged is None else merged + y
    y = alpha * h_ref[...] + _dot(merged.astype(BF16), wo_ref[...])
    out = _layer_norm(y, g_ref[...], b_ref[...])
    o_ref[...] = out
    obf_ref[...] = out.astype(BF16)


def _merge(outs, gates, h, w_branch, w_out, ln_g, ln_b, layer, alpha, tm=256):
    t, d = h.shape
    tm = min(tm, t)
    bw = outs[0].shape[1]
    once = pl.Buffered(1)
    return pl.pallas_call(
        functools.partial(_merge_body, alpha),
        grid=(t // tm,),
        in_specs=[pl.BlockSpec((tm, bw), lambda i: (i, 0))] * 4 + [
            pl.BlockSpec((tm, N_BRANCH * d), lambda i: (i, 0)),
            pl.BlockSpec((tm, d), lambda i: (i, 0)),
            pl.BlockSpec((None, N_BRANCH, bw, d), lambda i: (layer, 0, 0, 0), pipeline_mode=once),
            pl.BlockSpec((None, d, d), lambda i: (layer, 0, 0), pipeline_mode=once),
            pl.BlockSpec((None, None, 1, d), lambda i: (layer, 1, 0, 0)),
            pl.BlockSpec((None, None, 1, d), lambda i: (layer, 1, 0, 0)),
        ],
        out_specs=[pl.BlockSpec((tm, d), lambda i: (i, 0)), pl.BlockSpec((tm, d), lambda i: (i, 0))],
        out_shape=[jax.ShapeDtypeStruct((t, d), F32), jax.ShapeDtypeStruct((t, d), BF16)],
        compiler_params=_cparams(("parallel",)),
        name="merge",
    )(*outs, gates, h, w_branch, w_out, ln_g, ln_b)


def _rope_lane_tables(positions, rot_dim, group):
    inv = ROPE_THETA ** (-jnp.arange(0, rot_dim, 2, dtype=F32) / rot_dim)
    ang = positions.astype(F32)[..., None] * inv
    cos, sin = jnp.cos(ang), jnp.sin(ang)
    half = rot_dim // 2
    rest = group - rot_dim
    shape = cos.shape[:-1]
    ones = jnp.ones(shape + (rest,), F32)
    c = jnp.concatenate([cos, cos, ones], axis=-1)
    sa = jnp.concatenate([-sin, jnp.zeros(shape + (half + rest,), F32)], axis=-1)
    sb = jnp.concatenate([jnp.zeros(shape + (half,), F32), sin, jnp.zeros(shape + (rest,), F32)], axis=-1)
    reps = LANES // group
    return tuple(jnp.tile(t, (1, 1, reps)) for t in (c, sa, sb))


def _pad_last(x, width):
    return jnp.pad(x, [(0, 0)] * (x.ndim - 1) + [(0, width - x.shape[-1])])


def _prepare_in_proj(w_in):
    w_in = w_in.astype(BF16)
    offs = [0]
    for sz in IN_SPLITS:
        offs.append(offs[-1] + sz)
    seg = lambda i: w_in[..., offs[i]:offs[i + 1]]
    (a_q, a_k, a_v, b_q, b_k, b_v, b_glow, b_r, c_q, c_kv, c_kr, d_z, d_xbc, d_dt, gate) = [seg(i) for i in range(15)]
    lead = w_in.shape[:-1]

    def pad_heads(w):
        return _pad_last(w.reshape(lead + (GLA_HEADS, GLA_DK)), LANES).reshape(lead + (GLA_HEADS * LANES,))

    d_x, d_bc = d_xbc[..., :SSD_INNER], d_xbc[..., SSD_INNER:]
    w_mix = jnp.concatenate([c_q, c_kv, a_q, a_k, a_v, pad_heads(b_q), pad_heads(b_k), b_v, b_r,
                             d_bc, d_x, d_z, _pad_last(c_kr, LANES)], axis=-1)
    w_small = _pad_last(jnp.concatenate([b_glow, d_dt], axis=-1), LANES)
    return w_mix, w_small, gate


def kernel(x, positions, ln_g, ln_b, ffn1_w_gu, ffn1_w_down, ffn2_w_gu, ffn2_w_down, w_in, b_gate, diff_lambda, diff_subln_g, gla_w_gate2, gla_b_gate, gla_norm_g, mla_q_norm_g, mla_w_uq, mla_kv_norm_g, mla_w_ukv, ssd_conv_w, ssd_conv_b, ssd_dt_bias, ssd_a_log, ssd_d, ssd_norm_g, w_branch, w_out):
    bsz, s_len, d = x.shape
    depth = ln_g.shape[0]
    alpha = (2 * depth) ** 0.25
    t = bsz * s_len

    rope_d = _rope_lane_tables(positions, DIFF_ROT, DIFF_HEAD_DIM)
    rope_m = _rope_lane_tables(positions, MLA_ROPE, LANES)

    w1_gu, w1_dn = ffn1_w_gu.astype(BF16), ffn1_w_down.astype(BF16)
    w2_gu, w2_dn = ffn2_w_gu.astype(BF16), ffn2_w_down.astype(BF16)
    w_mix, w_small, w_gate = _prepare_in_proj(w_in)
    b_gate3 = b_gate[:, None, :]
    ln_g4, ln_b4 = ln_g[:, :, None, :], ln_b[:, :, None, :]
    subln3 = diff_subln_g[:, None, :]
    wg2 = gla_w_gate2.reshape(depth, GLA_GATE_RANK, GLA_HEADS, GLA_DK).transpose(0, 2, 1, 3)
    wg2 = jnp.pad(wg2, ((0, 0), (0, 0), (SMALL_GLOW, LANES - GLA_GATE_RANK - SMALL_GLOW), (0, LANES - GLA_DK))).astype(BF16)
    bg2 = _pad_last(gla_b_gate.reshape(depth, GLA_HEADS, 1, GLA_DK), LANES)
    gla_ng3 = gla_norm_g[:, None, :]
    wq = mla_w_uq.reshape(depth, MLA_Q_RANK, MLA_HEADS, MLA_NOPE + MLA_ROPE).transpose(0, 2, 1, 3)
    wq = _pad_last(wq, 2 * LANES).astype(BF16)
    wkv = mla_w_ukv.reshape(depth, MLA_KV_RANK, MLA_HEADS, MLA_NOPE + MLA_V).transpose(0, 2, 1, 3).astype(BF16)
    mla_qg3, mla_kvg3 = mla_q_norm_g[:, None, :], mla_kv_norm_g[:, None, :]
    dtb3 = jnp.pad(ssd_dt_bias, ((0, 0), (SMALL_DT, LANES - SMALL_DT - SSD_HEADS)))[:, None, :]
    alog3 = jnp.pad(ssd_a_log, ((0, 0), (SMALL_DT, LANES - SMALL_DT - SSD_HEADS)))[:, None, :]
    dsk3 = jnp.repeat(ssd_d, SSD_HEADDIM, axis=-1)[:, None, :]
    cb3 = ssd_conv_b[:, None, :]
    ssd_ng3 = ssd_norm_g[:, None, :]
    wbr = w_branch.astype(BF16)
    wout = w_out.astype(BF16)

    h = x.reshape(t, d)
    for l in range(depth):
        h, hb = _ffn(h, w1_gu, w1_dn, ln_g4, ln_b4, l, 0, alpha)
        p_mix, p_small = _inproj(hb, w_mix, w_small, l)
        gates = _gateproj(hb, w_gate, b_gate3, l)
        o_a = _diff_attention(p_mix, rope_d, diff_lambda, subln3, l, bsz, s_len)
        o_b = _gla(p_mix, p_small, wg2, bg2, gla_ng3, l, bsz, s_len)
        o_c = _mla_attention(p_mix, rope_m, mla_qg3, mla_kvg3, wq, wkv, l, bsz, s_len)
        o_d = _ssd(p_mix, p_small, ssd_conv_w, cb3, dtb3, alog3, dsk3, ssd_ng3, l, bsz, s_len)
        h, hb = _merge((o_a, o_b, o_c, o_d), gates, h, wbr, wout, ln_g4, ln_b4, l, alpha)
        h, hb = _ffn(h, w2_gu, w2_dn, ln_g4, ln_b4, l, 2, alpha)
    return h.reshape(bsz, s_len, d)
```

```python
import functools
import math

import jax
import jax.numpy as jnp
from jax import lax
from jax.experimental import pallas as pl
from jax.experimental.pallas import tpu as pltpu

F32 = jnp.float32
BF16 = jnp.bfloat16

D_MODEL = 2048
N_BRANCH = 4
BRANCH_WIDTH = D_MODEL // 4
ROPE_THETA = 500000.0
NORM_EPS = 1e-5

DIFF_HEADS = 4
DIFF_HEAD_DIM = 64
DIFF_ROT = 16

GLA_HEADS = 4
GLA_DK = 64
GLA_DV = 128
GLA_GATE_RANK = 16
GLA_TAU = 16.0
GLA_CHUNK = 64
GLA_UNROLL = 8

MLA_HEADS = 4
MLA_NOPE = 128
MLA_ROPE = 64
MLA_V = 128
MLA_Q_RANK = 384
MLA_KV_RANK = 128

SSD_HEADDIM = 64
SSD_INNER = 512
SSD_HEADS = 8
SSD_GROUPS = 2
SSD_STATE = 128
SSD_CONV = 4
SSD_CHUNK = 128

D_FF = 5632

IN_SPLITS = [512, 512, 512, 256, 256, 512, 16, 512, 384, 128, 64, 512, 1024, 8, 4 * D_MODEL]

LANES = 128
VMEM_LIMIT_BYTES = 56 * 2**20

MIX_CQ = 0
MIX_CKV = 3
MIX_AQ = 4
MIX_AK = 8
MIX_AV = 12
MIX_BQ = 16
MIX_BK = 20
MIX_BV = 24
MIX_BR = 28
MIX_SBC = 32
MIX_SX = 36
MIX_SZ = 40
MIX_CKR = 44
MIX_WIDTH = 45 * LANES
SMALL_GLOW = 0
SMALL_DT = 16

NEG_BIG = -1e30
LOG2E = math.log2(math.e)


def _cparams(semantics):
    return pltpu.CompilerParams(dimension_semantics=semantics, vmem_limit_bytes=VMEM_LIMIT_BYTES)


def _layer_norm(y, g, b):
    mu = jnp.mean(y, axis=-1, keepdims=True)
    yc = y - mu
    var = jnp.mean(yc * yc, axis=-1, keepdims=True)
    return yc * lax.rsqrt(var + NORM_EPS) * g + b


def _rms(x, g):
    return x * lax.rsqrt(jnp.mean(x * x, axis=-1, keepdims=True) + NORM_EPS) * g


def _sigmoid(x):
    return 1.0 / (1.0 + jnp.exp(-x))


def _silu(x):
    return x * _sigmoid(x)


def _softplus(x):
    return jnp.maximum(x, 0.0) + jnp.log(1.0 + jnp.exp(-jnp.abs(x)))


def _dot(a, b):
    return jnp.dot(a, b, preferred_element_type=F32)


def _dot_nt(a, b):
    return lax.dot_general(a, b, (((1,), (1,)), ((), ())), preferred_element_type=F32)


def _dot_tn(a, b):
    return lax.dot_general(a, b, (((0,), (0,)), ((), ())), preferred_element_type=F32)


def _ffn_body(alpha, nt, nj, h_ref, wg_ref, wu_ref, wd_ref, g_ref, b_ref, o_ref, ob_ref, hb_ref, acc_ref, fin_ref):
    i = pl.program_id(0)
    j = pl.program_id(1)
    first_j, last_j = j == 0, j == nj - 1

    def start_tile():
        h = h_ref[...]
        hb_ref[...] = h.astype(BF16)
        acc_ref[...] = (2.0 * alpha) * h

    def matmul_step(dst_ref):
        hb = hb_ref[...]
        gate = _dot(hb, wg_ref[...])
        up = _dot(hb, wu_ref[...])
        act = (_silu(gate) * up).astype(BF16)
        dst_ref[...] = acc_ref[...] + _dot(act, wd_ref[...])

    def finish():
        out = _layer_norm(0.5 * fin_ref[...], g_ref[...], b_ref[...])
        o_ref[...] = out
        ob_ref[...] = out.astype(BF16)

    @pl.when(jnp.logical_and(first_j, i == 0))
    def _():
        start_tile()
        matmul_step(acc_ref)

    @pl.when(jnp.logical_and(first_j, i > 0))
    def _():
        start_tile()
        finish()
        matmul_step(acc_ref)

    @pl.when(jnp.logical_and(j > 0, j < nj - 1))
    def _():
        matmul_step(acc_ref)

    @pl.when(jnp.logical_and(last_j, i < nt - 1))
    def _():
        matmul_step(fin_ref)

    @pl.when(jnp.logical_and(last_j, i == nt - 1))
    def _():
        matmul_step(fin_ref)
        finish()


def _ffn(h, w_gu, w_down, ln_g, ln_b, layer, ln_idx, alpha, tm=512, tf=512):
    t, d = h.shape
    dff = w_down.shape[1]
    nj = dff // tf
    nt = t // tm
    assert nj >= 2, "first and last d_ff chunk of a tile must be different grid steps"
    out_map = lambda i, j: (jnp.where(j == 0, jnp.maximum(i - 1, 0), i), 0)
    return pl.pallas_call(
        functools.partial(_ffn_body, alpha, nt, nj),
        grid=(nt, nj),
        in_specs=[
            pl.BlockSpec((tm, d), lambda i, j: (i, 0)),
            pl.BlockSpec((None, d, tf), lambda i, j: (layer, 0, j)),
            pl.BlockSpec((None, d, tf), lambda i, j: (layer, 0, nj + j)),
            pl.BlockSpec((None, tf, d), lambda i, j: (layer, j, 0)),
            pl.BlockSpec((None, None, 1, d), lambda i, j: (layer, ln_idx, 0, 0)),
            pl.BlockSpec((None, None, 1, d), lambda i, j: (layer, ln_idx, 0, 0)),
        ],
        out_specs=[pl.BlockSpec((tm, d), out_map), pl.BlockSpec((tm, d), out_map)],
        out_shape=[jax.ShapeDtypeStruct((t, d), F32), jax.ShapeDtypeStruct((t, d), BF16)],
        scratch_shapes=[pltpu.VMEM((tm, d), BF16), pltpu.VMEM((tm, d), F32), pltpu.VMEM((tm, d), F32)],
        compiler_params=_cparams(("arbitrary", "arbitrary")),
        name="ffn",
    )(h, w_gu, w_gu, w_down, ln_g, ln_b)


def _inproj_body(hb_ref, w_ref, ws_ref, p_ref, ps_ref):
    hb = hb_ref[...]
    p_ref[...] = _dot(hb, w_ref[...]).astype(BF16)

    @pl.when(pl.program_id(1) == 0)
    def _():
        ps_ref[...] = _dot(hb, ws_ref[...])


def _inproj(hb, w_mix, w_small, layer, tm=1024, tn=1920):
    t, d = hb.shape
    n = w_mix.shape[2]
    tm = min(tm, t)
    return pl.pallas_call(
        _inproj_body,
        grid=(t // tm, n // tn),
        in_specs=[
            pl.BlockSpec((tm, d), lambda i, j: (i, 0)),
            pl.BlockSpec((None, d, tn), lambda i, j: (layer, 0, j)),
            pl.BlockSpec((None, d, LANES), lambda i, j: (layer, 0, 0)),
        ],
        out_specs=[
            pl.BlockSpec((tm, tn), lambda i, j: (i, j)),
            pl.BlockSpec((tm, LANES), lambda i, j: (i, 0)),
        ],
        out_shape=[jax.ShapeDtypeStruct((t, n), BF16), jax.ShapeDtypeStruct((t, LANES), F32)],
        compiler_params=_cparams(("parallel", "arbitrary")),
        name="inproj",
    )(hb, w_mix, w_small)


def _gate_body(hb_ref, w_ref, b_ref, o_ref):
    z = _dot(hb_ref[...], w_ref[...]) + b_ref[...]
    o_ref[...] = _sigmoid(z).astype(BF16)


def _gateproj(hb, w_gate, b_gate, layer, tm=1024, tn=2048):
    t, d = hb.shape
    n = w_gate.shape[2]
    tm = min(tm, t)
    return pl.pallas_call(
        _gate_body,
        grid=(t // tm, n // tn),
        in_specs=[
            pl.BlockSpec((tm, d), lambda i, j: (i, 0)),
            pl.BlockSpec((None, d, tn), lambda i, j: (layer, 0, j)),
            pl.BlockSpec((None, 1, tn), lambda i, j: (layer, 0, j)),
        ],
        out_specs=pl.BlockSpec((tm, tn), lambda i, j: (i, j)),
        out_shape=jax.ShapeDtypeStruct((t, n), BF16),
        compiler_params=_cparams(("parallel", "arbitrary")),
        name="gateproj",
    )(hb, w_gate, b_gate)


def _rope(x, c, sa, sb, half):
    return x * c + pltpu.roll(x, LANES - half, 1) * sa + pltpu.roll(x, half, 1) * sb


def _causal_block(q_rows, ks_ref, vs_ref, blk, tq, n_maps):
    length = (blk + 1) * tq
    s = _dot_nt(q_rows, ks_ref[0:length, :])
    rows = lax.broadcasted_iota(jnp.int32, s.shape, 0)
    if n_maps > 1:
        rows = jnp.where(rows >= tq, rows - tq, rows)
    cols = lax.broadcasted_iota(jnp.int32, s.shape, 1)
    s = jnp.where(cols <= rows + blk * tq, s, NEG_BIG)
    m = jnp.max(s, axis=-1, keepdims=True)
    p = jnp.exp2(s - m)
    l = jnp.sum(p, axis=-1, keepdims=True)
    o = _dot(p.astype(BF16), vs_ref[0:length, :])
    return o / l


def _diff_body(lam_init, tq, q_ref, k_ref, v_ref, c_ref, sa_ref, sb_ref, lam_ref, g_ref, o_ref, qs_ref, ks_ref):
    s_len = q_ref.shape[0]
    c, sa, sb = c_ref[...], sa_ref[...], sb_ref[...]
    q = _rope(q_ref[...].astype(F32), c, sa, sb, DIFF_ROT // 2) * (DIFF_HEAD_DIM ** -0.5 * LOG2E)
    k = _rope(k_ref[...].astype(F32), c, sa, sb, DIFF_ROT // 2)
    first = lax.broadcasted_iota(jnp.int32, (1, LANES), 1) < DIFF_HEAD_DIM
    qs_ref[0] = jnp.where(first, q, 0.0).astype(BF16)
    qs_ref[1] = jnp.where(first, 0.0, q).astype(BF16)
    ks_ref[...] = k.astype(BF16)

    lp = lam_ref[...]
    lam = (jnp.exp(jnp.sum(lp[0:1] * lp[1:2], axis=-1, keepdims=True))
           - jnp.exp(jnp.sum(lp[2:3] * lp[3:4], axis=-1, keepdims=True)) + lam_init)
    g = g_ref[...]
    for blk in range(s_len // tq):
        r0 = blk * tq
        q_rows = jnp.concatenate([qs_ref[0, r0:r0 + tq, :], qs_ref[1, r0:r0 + tq, :]], axis=0)
        o = _causal_block(q_rows, ks_ref, v_ref, blk, tq, 2)
        od = o[:tq] - lam * o[tq:]
        o_ref[r0:r0 + tq, :] = (_rms(od, g) * (1.0 - lam_init)).astype(BF16)


def _diff_attention(p_mix, rope_d, lam, subln_g, layer, bsz, s_len, tq=512):
    t = p_mix.shape[0]
    lam_init = 0.8 - 0.6 * math.exp(-0.3 * layer)
    tq = min(tq, s_len)
    blk = lambda off: pl.BlockSpec((s_len, LANES), lambda b, h: (b, off + h))
    tab = pl.BlockSpec((None, s_len, LANES), lambda b, h: (b, 0, 0))
    return pl.pallas_call(
        functools.partial(_diff_body, lam_init, tq),
        grid=(bsz, DIFF_HEADS),
        in_specs=[blk(MIX_AQ), blk(MIX_AK), blk(MIX_AV), tab, tab, tab,
                  pl.BlockSpec((None, 4, DIFF_HEAD_DIM), lambda b, h: (layer, 0, 0)),
                  pl.BlockSpec((None, 1, LANES), lambda b, h: (layer, 0, 0))],
        out_specs=pl.BlockSpec((s_len, LANES), lambda b, h: (b, h)),
        out_shape=jax.ShapeDtypeStruct((t, BRANCH_WIDTH), BF16),
        scratch_shapes=[pltpu.VMEM((2, s_len, LANES), BF16), pltpu.VMEM((s_len, LANES), BF16)],
        compiler_params=_cparams(("parallel", "arbitrary")),
        name="diff_attn",
    )(p_mix, p_mix, p_mix, *rope_d, lam, subln_g)


def _mla_body(tq, cq_ref, ckv_ref, kr_ref, c_ref, sa_ref, sb_ref, qg_ref, kvg_ref, wq_ref, wkv_ref,
              o_ref, qs_ref, ks_ref, vs_ref):
    s_len = cq_ref.shape[0]
    c, sa, sb = c_ref[...], sa_ref[...], sb_ref[...]
    scale = (MLA_NOPE + MLA_ROPE) ** -0.5 * LOG2E
    cqn = _rms(cq_ref[...].astype(F32), qg_ref[...]).astype(BF16)
    qc = _dot(cqn, wq_ref[...])
    qs_ref[:, 0:LANES] = (qc[:, 0:LANES] * scale).astype(BF16)
    qs_ref[:, LANES:] = (_rope(qc[:, LANES:], c, sa, sb, MLA_ROPE // 2) * scale).astype(BF16)
    ckvn = _rms(ckv_ref[...].astype(F32), kvg_ref[...]).astype(BF16)
    kv = _dot(ckvn, wkv_ref[...])
    ks_ref[:, 0:LANES] = kv[:, 0:LANES].astype(BF16)
    ks_ref[:, LANES:] = _rope(kr_ref[...].astype(F32), c, sa, sb, MLA_ROPE // 2).astype(BF16)
    vs_ref[...] = kv[:, LANES:].astype(BF16)
    for blk in range(s_len // tq):
        r0 = blk * tq
        o = _causal_block(qs_ref[r0:r0 + tq, :], ks_ref, vs_ref, blk, tq, 1)
        o_ref[r0:r0 + tq, :] = o.astype(BF16)


def _mla_attention(p_mix, rope_m, q_norm_g, kv_norm_g, w_uq, w_ukv, layer, bsz, s_len, tq=512):
    t = p_mix.shape[0]
    tq = min(tq, s_len)
    tab = pl.BlockSpec((None, s_len, LANES), lambda b, h: (b, 0, 0))
    return pl.pallas_call(
        functools.partial(_mla_body, tq),
        grid=(bsz, MLA_HEADS),
        in_specs=[pl.BlockSpec((s_len, MLA_Q_RANK), lambda b, h: (b, MIX_CQ)),
                  pl.BlockSpec((s_len, LANES), lambda b, h: (b, MIX_CKV)),
                  pl.BlockSpec((s_len, LANES), lambda b, h: (b, MIX_CKR)),
                  tab, tab, tab,
                  pl.BlockSpec((None, 1, MLA_Q_RANK), lambda b, h: (layer, 0, 0)),
                  pl.BlockSpec((None, 1, MLA_KV_RANK), lambda b, h: (layer, 0, 0)),
                  pl.BlockSpec((None, None, MLA_Q_RANK, 2 * LANES), lambda b, h: (layer, h, 0, 0)),
                  pl.BlockSpec((None, None, MLA_KV_RANK, 2 * LANES), lambda b, h: (layer, h, 0, 0))],
        out_specs=pl.BlockSpec((s_len, LANES), lambda b, h: (b, h)),
        out_shape=jax.ShapeDtypeStruct((t, BRANCH_WIDTH), BF16),
        scratch_shapes=[pltpu.VMEM((s_len, 2 * LANES), BF16), pltpu.VMEM((s_len, 2 * LANES), BF16),
                        pltpu.VMEM((s_len, LANES), BF16)],
        compiler_params=_cparams(("parallel", "arbitrary")),
        name="mla_attn",
    )(p_mix, p_mix, p_mix, *rope_m, q_norm_g, kv_norm_g, w_uq, w_ukv)


def _chunk_cumsum(x, chunk):
    pos = lax.broadcasted_iota(jnp.int32, x.shape, 0) % chunk
    step = 1
    while step < chunk:
        x = x + jnp.where(pos >= step, pltpu.roll(x, step, 0), 0.0)
        step *= 2
    return x


def _gla_body(q_ref, k_ref, v_ref, r_ref, ps_ref, wg_ref, bg_ref, ng_ref, o_ref, b_ref):
    s_len = q_ref.shape[0]
    ck = GLA_CHUNK
    x = _dot(ps_ref[...].astype(BF16), wg_ref[...]) + bg_ref[...]
    g = (jnp.minimum(x, 0.0) - jnp.log(1.0 + jnp.exp(-jnp.abs(x)))) / GLA_TAU
    b_ref[...] = _chunk_cumsum(g, ck)
    tril = (lax.broadcasted_iota(jnp.int32, (ck, ck), 0) >= lax.broadcasted_iota(jnp.int32, (ck, ck), 1))
    ng = ng_ref[...]

    def chunk(ci, st):
        sl = pl.ds(pl.multiple_of(ci * ck, ck), ck)
        bc = b_ref[sl, :]
        bl = b_ref[pl.ds(ci * ck + ck - 1, 1), :]
        kc = k_ref[sl, :].astype(F32)
        qt = (q_ref[sl, :].astype(F32) * (GLA_DK ** -0.5) * jnp.exp(bc)).astype(BF16)
        kt = (kc * jnp.exp(-bc)).astype(BF16)
        kd = (kc * jnp.exp(bl - bc)).astype(BF16)
        vc = v_ref[sl, :]
        att = jnp.where(tril, _dot_nt(qt, kt), 0.0).astype(BF16)
        o = _dot(att, vc) + _dot_nt(qt, st.astype(BF16))
        o_ref[sl, :] = (_rms(o, ng) * _silu(r_ref[sl, :].astype(F32))).astype(BF16)
        return jnp.exp(bl) * st + _dot_tn(vc, kd)

    n_chunks = s_len // ck
    lax.fori_loop(0, n_chunks, chunk, jnp.zeros((GLA_DV, LANES), F32), unroll=math.gcd(n_chunks, GLA_UNROLL))


def _gla(p_mix, p_small, w_gate2, b_gate, norm_g, layer, bsz, s_len):
    t = p_mix.shape[0]
    blk = lambda off: pl.BlockSpec((s_len, LANES), lambda b, h: (b, off + h))
    return pl.pallas_call(
        _gla_body,
        grid=(bsz, GLA_HEADS),
        in_specs=[blk(MIX_BQ), blk(MIX_BK), blk(MIX_BV), blk(MIX_BR),
                  pl.BlockSpec((s_len, LANES), lambda b, h: (b, 0)),
                  pl.BlockSpec((None, None, LANES, LANES), lambda b, h: (layer, h, 0, 0)),
                  pl.BlockSpec((None, None, 1, LANES), lambda b, h: (layer, h, 0, 0)),
                  pl.BlockSpec((None, 1, LANES), lambda b, h: (layer, 0, 0))],
        out_specs=pl.BlockSpec((s_len, LANES), lambda b, h: (b, h)),
        out_shape=jax.ShapeDtypeStruct((t, BRANCH_WIDTH), BF16),
        scratch_shapes=[pltpu.VMEM((s_len, LANES), F32)],
        compiler_params=_cparams(("parallel", "arbitrary")),
        name="gla",
    )(p_mix, p_mix, p_mix, p_mix, p_small, w_gate2, b_gate, norm_g)


def _causal_conv_silu(x, w, b):
    row = lax.broadcasted_iota(jnp.int32, x.shape, 0)
    y = x * w[SSD_CONV - 1:SSD_CONV] + b
    for shift in range(1, SSD_CONV):
        xs = jnp.where(row >= shift, pltpu.roll(x, shift, 0), 0.0)
        y = y + xs * w[SSD_CONV - 1 - shift:SSD_CONV - shift]
    return _silu(y)


def _lane_col(a, idx, rows):
    return jnp.broadcast_to(a[:, idx:idx + 1], (rows, LANES))


def _ssd_body(bc_ref, x_ref, z_ref, ps_ref, cw_bc_ref, cw_x_ref, cb_bc_ref, cb_x_ref, dtb_ref, alog_ref,
              dsk_ref, ng_ref, o_ref, xs_ref, bs_ref, cs_ref, dt_ref, ac_ref, st_ref):
    s_len = x_ref.shape[0]
    ck = SSD_CHUNK
    n_pairs = SSD_HEADS // 2
    xs_ref[...] = _causal_conv_silu(x_ref[...].astype(F32), cw_x_ref[...], cb_x_ref[...])
    bcv = _causal_conv_silu(bc_ref[...].astype(F32), cw_bc_ref[...], cb_bc_ref[...])
    bs_ref[...] = bcv[:, 0:SSD_GROUPS * SSD_STATE].astype(BF16)
    cs_ref[...] = bcv[:, SSD_GROUPS * SSD_STATE:].astype(BF16)
    dt = _softplus(ps_ref[...] + dtb_ref[...])
    dt_ref[...] = dt
    ac_ref[...] = _chunk_cumsum(dt * (-jnp.exp(alog_ref[...])), ck)
    st_ref[...] = jnp.zeros_like(st_ref)

    tril = (lax.broadcasted_iota(jnp.int32, (ck, ck), 0) >= lax.broadcasted_iota(jnp.int32, (ck, ck), 1))
    first = lax.broadcasted_iota(jnp.int32, (1, LANES), 1) < SSD_HEADDIM

    def chunk(ci, carry):
        sl = pl.ds(pl.multiple_of(ci * ck, ck), ck)
        a = ac_ref[sl, :]
        a_t = a.T
        a_last = ac_ref[pl.ds(ci * ck + ck - 1, 1), :]
        dtc = dt_ref[sl, :]
        for grp in range(SSD_GROUPS):
            bm = bs_ref[sl, grp * SSD_STATE:(grp + 1) * SSD_STATE]
            cm = cs_ref[sl, grp * SSD_STATE:(grp + 1) * SSD_STATE]
            cb = _dot_nt(cm, bm)
            ys = []
            for pp in range(n_pairs // SSD_GROUPS):
                pair = grp * (n_pairs // SSD_GROUPS) + pp
                h0 = SMALL_DT + 2 * pair
                a0, a1 = _lane_col(a, h0, ck), _lane_col(a, h0 + 1, ck)
                a_pair = jnp.where(first, a0, a1)
                dt_pair = jnp.where(first, _lane_col(dtc, h0, ck), _lane_col(dtc, h0 + 1, ck))
                al_pair = jnp.where(first, _lane_col(a_last, h0, 1), _lane_col(a_last, h0 + 1, 1))
                xp = xs_ref[sl, pair * LANES:(pair + 1) * LANES]
                xdt = xp * dt_pair
                m0 = (cb * jnp.where(tril, jnp.exp(a0 - a_t[h0:h0 + 1, :]), 0.0)).astype(BF16)
                m1 = (cb * jnp.where(tril, jnp.exp(a1 - a_t[h0 + 1:h0 + 2, :]), 0.0)).astype(BF16)
                y = (_dot(m0, jnp.where(first, xdt, 0.0).astype(BF16))
                     + _dot(m1, jnp.where(first, 0.0, xdt).astype(BF16)))
                st = st_ref[pair]
                y = y + _dot(cm, st.astype(BF16)) * jnp.exp(a_pair)
                st_ref[pair] = jnp.exp(al_pair) * st + _dot_tn(bm, (xdt * jnp.exp(al_pair - a_pair)).astype(BF16))
                ys.append(y + dsk_ref[:, pair * LANES:(pair + 1) * LANES] * xp)
            width = SSD_INNER // SSD_GROUPS
            lo = grp * width
            yg = jnp.concatenate(ys, axis=1) * _silu(z_ref[sl, lo:lo + width].astype(F32))
            o_ref[sl, lo:lo + width] = _rms(yg, ng_ref[:, lo:lo + width]).astype(BF16)
        return carry

    n_chunks = s_len // ck
    lax.fori_loop(0, n_chunks, chunk, 0, unroll=math.gcd(n_chunks, 2))


def _ssd(p_mix, p_small, conv_w, conv_b, dt_bias, a_log, d_skip, norm_g, layer, bsz, s_len):
    t = p_mix.shape[0]
    wide = SSD_INNER
    row = lambda width, idx: pl.BlockSpec((s_len, width), lambda b: (b, idx))
    par = lambda rows, width, idx: pl.BlockSpec((None, rows, width), lambda b: (layer, 0, idx))
    return pl.pallas_call(
        _ssd_body,
        grid=(bsz,),
        in_specs=[row(wide, MIX_SBC * LANES // wide), row(wide, MIX_SX * LANES // wide),
                  row(wide, MIX_SZ * LANES // wide), row(LANES, 0),
                  par(SSD_CONV, wide, 1), par(SSD_CONV, wide, 0), par(1, wide, 1), par(1, wide, 0),
                  par(1, LANES, 0), par(1, LANES, 0), par(1, wide, 0), par(1, wide, 0)],
        out_specs=pl.BlockSpec((s_len, wide), lambda b: (b, 0)),
        out_shape=jax.ShapeDtypeStruct((t, wide), BF16),
        scratch_shapes=[pltpu.VMEM((s_len, wide), F32), pltpu.VMEM((s_len, SSD_GROUPS * SSD_STATE), BF16),
                        pltpu.VMEM((s_len, SSD_GROUPS * SSD_STATE), BF16), pltpu.VMEM((s_len, LANES), F32),
                        pltpu.VMEM((s_len, LANES), F32), pltpu.VMEM((SSD_HEADS // 2, SSD_STATE, LANES), F32)],
        compiler_params=_cparams(("parallel",)),
        name="ssd",
    )(p_mix, p_mix, p_mix, p_small, conv_w, conv_w, conv_b, conv_b, dt_bias, a_log, d_skip, norm_g)


def _merge_body(alpha, oa_ref, ob_ref, oc_ref, od_ref, gt_ref, h_ref, wb_ref, wo_ref, g_ref, b_ref, o_ref, obf_ref,
                y_ref):
    d = h_ref.shape[1]

    @pl.when(pl.program_id(0) == 0)
    def _():
        y_ref[...] = jnp.zeros_like(y_ref)

    out = _layer_norm(y_ref[...], g_ref[...], b_ref[...])
    o_ref[...] = out
    obf_ref[...] = out.astype(BF16)
    merged = None
    for n, br in enumerate((oa_ref, ob_ref, oc_ref, od_ref)):
        y = gt_ref[:, n * d:(n + 1) * d].astype(F32) * _dot(br[...], wb_ref[n])
        merged = y if merged is None else merged + y
    y_ref[...] = alpha * h_ref[...] + _dot(merged.astype(BF16), wo_ref[...])


def _merge(outs, gates, h, w_branch, w_out, ln_g, ln_b, layer, alpha, tm=256):
    t, d = h.shape
    tm = min(tm, t)
    nt = t // tm
    bw = outs[0].shape[1]
    once = pl.Buffered(1)
    in_map = lambda i: (jnp.minimum(i, nt - 1), 0)
    out_map = lambda i: (jnp.maximum(i - 1, 0), 0)
    return pl.pallas_call(
        functools.partial(_merge_body, alpha),
        grid=(nt + 1,),
        in_specs=[pl.BlockSpec((tm, bw), in_map)] * 4 + [
            pl.BlockSpec((tm, N_BRANCH * d), in_map),
            pl.BlockSpec((tm, d), in_map),
            pl.BlockSpec((None, N_BRANCH, bw, d), lambda i: (layer, 0, 0, 0), pipeline_mode=once),
            pl.BlockSpec((None, d, d), lambda i: (layer, 0, 0), pipeline_mode=once),
            pl.BlockSpec((None, None, 1, d), lambda i: (layer, 1, 0, 0)),
            pl.BlockSpec((None, None, 1, d), lambda i: (layer, 1, 0, 0)),
        ],
        out_specs=[pl.BlockSpec((tm, d), out_map), pl.BlockSpec((tm, d), out_map)],
        out_shape=[jax.ShapeDtypeStruct((t, d), F32), jax.ShapeDtypeStruct((t, d), BF16)],
        scratch_shapes=[pltpu.VMEM((tm, d), F32)],
        compiler_params=_cparams(("arbitrary",)),
        name="merge",
    )(*outs, gates, h, w_branch, w_out, ln_g, ln_b)


def _rope_lane_tables(positions, rot_dim, group):
    inv = ROPE_THETA ** (-jnp.arange(0, rot_dim, 2, dtype=F32) / rot_dim)
    ang = positions.astype(F32)[..., None] * inv
    cos, sin = jnp.cos(ang), jnp.sin(ang)
    half = rot_dim // 2
    rest = group - rot_dim
    shape = cos.shape[:-1]
    ones = jnp.ones(shape + (rest,), F32)
    c = jnp.concatenate([cos, cos, ones], axis=-1)
    sa = jnp.concatenate([-sin, jnp.zeros(shape + (half + rest,), F32)], axis=-1)
    sb = jnp.concatenate([jnp.zeros(shape + (half,), F32), sin, jnp.zeros(shape + (rest,), F32)], axis=-1)
    reps = LANES // group
    return tuple(jnp.tile(t, (1, 1, reps)) for t in (c, sa, sb))


def _pad_last(x, width):
    return jnp.pad(x, [(0, 0)] * (x.ndim - 1) + [(0, width - x.shape[-1])])


def _prepare_in_proj(w_in):
    w_in = w_in.astype(BF16)
    offs = [0]
    for sz in IN_SPLITS:
        offs.append(offs[-1] + sz)
    seg = lambda i: w_in[..., offs[i]:offs[i + 1]]
    (a_q, a_k, a_v, b_q, b_k, b_v, b_glow, b_r, c_q, c_kv, c_kr, d_z, d_xbc, d_dt, gate) = [seg(i) for i in range(15)]
    lead = w_in.shape[:-1]

    def pad_heads(w):
        return _pad_last(w.reshape(lead + (GLA_HEADS, GLA_DK)), LANES).reshape(lead + (GLA_HEADS * LANES,))

    d_x, d_bc = d_xbc[..., :SSD_INNER], d_xbc[..., SSD_INNER:]
    w_mix = jnp.concatenate([c_q, c_kv, a_q, a_k, a_v, pad_heads(b_q), pad_heads(b_k), b_v, b_r,
                             d_bc, d_x, d_z, _pad_last(c_kr, LANES)], axis=-1)
    w_small = _pad_last(jnp.concatenate([b_glow, d_dt], axis=-1), LANES)
    return w_mix, w_small, gate


def kernel(x, positions, ln_g, ln_b, ffn1_w_gu, ffn1_w_down, ffn2_w_gu, ffn2_w_down, w_in, b_gate, diff_lambda, diff_subln_g, gla_w_gate2, gla_b_gate, gla_norm_g, mla_q_norm_g, mla_w_uq, mla_kv_norm_g, mla_w_ukv, ssd_conv_w, ssd_conv_b, ssd_dt_bias, ssd_a_log, ssd_d, ssd_norm_g, w_branch, w_out):
    bsz, s_len, d = x.shape
    depth = ln_g.shape[0]
    alpha = (2 * depth) ** 0.25
    t = bsz * s_len

    rope_d = _rope_lane_tables(positions, DIFF_ROT, DIFF_HEAD_DIM)
    rope_m = _rope_lane_tables(positions, MLA_ROPE, LANES)

    w1_gu, w1_dn = ffn1_w_gu.astype(BF16), ffn1_w_down.astype(BF16)
    w2_gu, w2_dn = ffn2_w_gu.astype(BF16), ffn2_w_down.astype(BF16)
    w_mix, w_small, w_gate = _prepare_in_proj(w_in)
    b_gate3 = b_gate[:, None, :]
    ln_g4, ln_b4 = ln_g[:, :, None, :], ln_b[:, :, None, :]
    subln3 = diff_subln_g[:, None, :]
    wg2 = gla_w_gate2.reshape(depth, GLA_GATE_RANK, GLA_HEADS, GLA_DK).transpose(0, 2, 1, 3)
    wg2 = jnp.pad(wg2, ((0, 0), (0, 0), (SMALL_GLOW, LANES - GLA_GATE_RANK - SMALL_GLOW), (0, LANES - GLA_DK))).astype(BF16)
    bg2 = _pad_last(gla_b_gate.reshape(depth, GLA_HEADS, 1, GLA_DK), LANES)
    gla_ng3 = gla_norm_g[:, None, :]
    wq = mla_w_uq.reshape(depth, MLA_Q_RANK, MLA_HEADS, MLA_NOPE + MLA_ROPE).transpose(0, 2, 1, 3)
    wq = _pad_last(wq, 2 * LANES).astype(BF16)
    wkv = mla_w_ukv.reshape(depth, MLA_KV_RANK, MLA_HEADS, MLA_NOPE + MLA_V).transpose(0, 2, 1, 3).astype(BF16)
    mla_qg3, mla_kvg3 = mla_q_norm_g[:, None, :], mla_kv_norm_g[:, None, :]
    dtb3 = jnp.pad(ssd_dt_bias, ((0, 0), (SMALL_DT, LANES - SMALL_DT - SSD_HEADS)))[:, None, :]
    alog3 = jnp.pad(ssd_a_log, ((0, 0), (SMALL_DT, LANES - SMALL_DT - SSD_HEADS)))[:, None, :]
    dsk3 = jnp.repeat(ssd_d, SSD_HEADDIM, axis=-1)[:, None, :]
    cb3 = ssd_conv_b[:, None, :]
    ssd_ng3 = ssd_norm_g[:, None, :]
    wbr = w_branch.astype(BF16)
    wout = w_out.astype(BF16)

    h = x.reshape(t, d)
    for l in range(depth):
        h, hb = _ffn(h, w1_gu, w1_dn, ln_g4, ln_b4, l, 0, alpha)
        p_mix, p_small = _inproj(hb, w_mix, w_small, l)
        gates = _gateproj(hb, w_gate, b_gate3, l)
        o_a = _diff_attention(p_mix, rope_d, diff_lambda, subln3, l, bsz, s_len)
        o_b = _gla(p_mix, p_small, wg2, bg2, gla_ng3, l, bsz, s_len)
        o_c = _mla_attention(p_mix, rope_m, mla_qg3, mla_kvg3, wq, wkv, l, bsz, s_len)
        o_d = _ssd(p_mix, p_small, ssd_conv_w, cb3, dtb3, alog3, dsk3, ssd_ng3, l, bsz, s_len)
        h, hb = _merge((o_a, o_b, o_c, o_d), gates, h, wbr, wout, ln_g4, ln_b4, l, alpha)
        h, hb = _ffn(h, w2_gu, w2_dn, ln_g4, ln_b4, l, 2, alpha)
    return h.reshape(bsz, s_len, d)
```

```python
import functools
import math

import jax
import jax.numpy as jnp
from jax import lax
from jax.experimental import pallas as pl
from jax.experimental.pallas import tpu as pltpu

F32 = jnp.float32
BF16 = jnp.bfloat16

D_MODEL = 2048
N_BRANCH = 4
BRANCH_WIDTH = D_MODEL // 4
ROPE_THETA = 500000.0
NORM_EPS = 1e-5

DIFF_HEADS = 4
DIFF_HEAD_DIM = 64
DIFF_ROT = 16

GLA_HEADS = 4
GLA_DK = 64
GLA_DV = 128
GLA_GATE_RANK = 16
GLA_TAU = 16.0
GLA_CHUNK = 64
GLA_UNROLL = 8

MLA_HEADS = 4
MLA_NOPE = 128
MLA_ROPE = 64
MLA_V = 128
MLA_Q_RANK = 384
MLA_KV_RANK = 128

SSD_HEADDIM = 64
SSD_INNER = 512
SSD_HEADS = 8
SSD_GROUPS = 2
SSD_STATE = 128
SSD_CONV = 4
SSD_CHUNK = 128

D_FF = 5632

IN_SPLITS = [512, 512, 512, 256, 256, 512, 16, 512, 384, 128, 64, 512, 1024, 8, 4 * D_MODEL]

LANES = 128
VMEM_LIMIT_BYTES = 63 * 2**20
FFN_TM = 1024
DIFF_HEADS_PER_STEP = 2
MLA_HEADS_PER_STEP = 1

MIX_CQ = 0
MIX_CKV = 3
MIX_AQ = 4
MIX_AK = 8
MIX_AV = 12
MIX_BQ = 16
MIX_BK = 20
MIX_BV = 24
MIX_BR = 28
MIX_SBC = 32
MIX_SX = 36
MIX_SZ = 40
MIX_CKR = 44
MIX_WIDTH = 45 * LANES
SMALL_GLOW = 0
SMALL_DT = 16

NEG_BIG = -1e30
LOG2E = math.log2(math.e)


def _cparams(semantics):
    return pltpu.CompilerParams(dimension_semantics=semantics, vmem_limit_bytes=VMEM_LIMIT_BYTES)


def _layer_norm(y, g, b):
    mu = jnp.mean(y, axis=-1, keepdims=True)
    yc = y - mu
    var = jnp.mean(yc * yc, axis=-1, keepdims=True)
    return yc * lax.rsqrt(var + NORM_EPS) * g + b


def _rms(x, g):
    return x * lax.rsqrt(jnp.mean(x * x, axis=-1, keepdims=True) + NORM_EPS) * g


def _sigmoid(x):
    return 1.0 / (1.0 + jnp.exp(-x))


def _silu(x):
    return x * _sigmoid(x)


def _softplus(x):
    return jnp.maximum(x, 0.0) + jnp.log(1.0 + jnp.exp(-jnp.abs(x)))


def _dot(a, b):
    return jnp.dot(a, b, preferred_element_type=F32)


def _dot_nt(a, b):
    return lax.dot_general(a, b, (((1,), (1,)), ((), ())), preferred_element_type=F32)


def _dot_tn(a, b):
    return lax.dot_general(a, b, (((0,), (0,)), ((), ())), preferred_element_type=F32)


def _ffn_body(alpha, nj, h_ref, wg_ref, wu_ref, wd_ref, g_ref, b_ref, o_ref, hb_ref):
    j = pl.program_id(1)

    @pl.when(j == 0)
    def _():
        h = h_ref[...]
        hb_ref[...] = h.astype(BF16)
        o_ref[...] = (2.0 * alpha) * h

    hb = hb_ref[...]
    gate = _dot(hb, wg_ref[...])
    up = _dot(hb, wu_ref[...])
    act = (_silu(gate) * up).astype(BF16)
    o_ref[...] += _dot(act, wd_ref[...])

    @pl.when(j == nj - 1)
    def _():
        o_ref[...] = _layer_norm(0.5 * o_ref[...], g_ref[...], b_ref[...])


def _ffn(h, w_gu, w_down, ln_g, ln_b, layer, ln_idx, alpha, tm=FFN_TM, tf=512):
    t, d = h.shape
    dff = w_down.shape[1]
    nj = dff // tf
    tm = min(tm, t)
    grid = (t // tm, nj)
    return pl.pallas_call(
        functools.partial(_ffn_body, alpha, nj),
        grid=grid,
        in_specs=[
            pl.BlockSpec((tm, d), lambda i, j: (i, 0)),
            pl.BlockSpec((None, d, tf), lambda i, j: (layer, 0, j)),
            pl.BlockSpec((None, d, tf), lambda i, j: (layer, 0, nj + j)),
            pl.BlockSpec((None, tf, d), lambda i, j: (layer, j, 0)),
            pl.BlockSpec((None, None, 1, d), lambda i, j: (layer, ln_idx, 0, 0)),
            pl.BlockSpec((None, None, 1, d), lambda i, j: (layer, ln_idx, 0, 0)),
        ],
        out_specs=pl.BlockSpec((tm, d), lambda i, j: (i, 0)),
        out_shape=jax.ShapeDtypeStruct((t, d), F32),
        scratch_shapes=[pltpu.VMEM((tm, d), BF16)],
        compiler_params=_cparams(("parallel", "arbitrary")),
        name="ffn",
    )(h, w_gu, w_gu, w_down, ln_g, ln_b)


def _inproj_body(h_ref, w_ref, ws_ref, p_ref, ps_ref, hb_ref):
    @pl.when(pl.program_id(1) == 0)
    def _():
        hb_ref[...] = h_ref[...].astype(BF16)
        ps_ref[...] = _dot(hb_ref[...], ws_ref[...])

    p_ref[...] = _dot(hb_ref[...], w_ref[...]).astype(BF16)


def _inproj(h, w_mix, w_small, layer, tm=1024, tn=1920):
    t, d = h.shape
    n = w_mix.shape[2]
    tm = min(tm, t)
    return pl.pallas_call(
        _inproj_body,
        grid=(t // tm, n // tn),
        in_specs=[
            pl.BlockSpec((tm, d), lambda i, j: (i, 0)),
            pl.BlockSpec((None, d, tn), lambda i, j: (layer, 0, j)),
            pl.BlockSpec((None, d, LANES), lambda i, j: (layer, 0, 0)),
        ],
        out_specs=[
            pl.BlockSpec((tm, tn), lambda i, j: (i, j)),
            pl.BlockSpec((tm, LANES), lambda i, j: (i, 0)),
            pl.BlockSpec((tm, d), lambda i, j: (i, 0)),
        ],
        out_shape=[jax.ShapeDtypeStruct((t, n), BF16), jax.ShapeDtypeStruct((t, LANES), F32),
                   jax.ShapeDtypeStruct((t, d), BF16)],
        compiler_params=_cparams(("parallel", "arbitrary")),
        name="inproj",
    )(h, w_mix, w_small)


def _gate_body(hb_ref, w_ref, b_ref, o_ref):
    z = _dot(hb_ref[...], w_ref[...]) + b_ref[...]
    o_ref[...] = _sigmoid(z).astype(BF16)


def _gateproj(hb, w_gate, b_gate, layer, tm=1024, tn=2048):
    t, d = hb.shape
    n = w_gate.shape[2]
    tm = min(tm, t)
    return pl.pallas_call(
        _gate_body,
        grid=(t // tm, n // tn),
        in_specs=[
            pl.BlockSpec((tm, d), lambda i, j: (i, 0)),
            pl.BlockSpec((None, d, tn), lambda i, j: (layer, 0, j)),
            pl.BlockSpec((None, 1, tn), lambda i, j: (layer, 0, j)),
        ],
        out_specs=pl.BlockSpec((tm, tn), lambda i, j: (i, j)),
        out_shape=jax.ShapeDtypeStruct((t, n), BF16),
        compiler_params=_cparams(("parallel", "arbitrary")),
        name="gateproj",
    )(hb, w_gate, b_gate)


def _rope(x, c, sa, sb, half):
    return x * c + pltpu.roll(x, LANES - half, 1) * sa + pltpu.roll(x, half, 1) * sb


def _causal_block(q_rows, k, v, blk, tq, n_maps):
    s = _dot_nt(q_rows, k)
    rows = lax.broadcasted_iota(jnp.int32, s.shape, 0)
    if n_maps > 1:
        rows = jnp.where(rows >= tq, rows - tq, rows)
    cols = lax.broadcasted_iota(jnp.int32, s.shape, 1)
    s = jnp.where(cols <= rows + blk * tq, s, NEG_BIG)
    m = jnp.max(s, axis=-1, keepdims=True)
    p = jnp.exp2(s - m)
    l = jnp.sum(p, axis=-1, keepdims=True)
    return _dot(p.astype(BF16), v) / l


def _diff_body(lam_init, tq, q_ref, k_ref, v_ref, c_ref, sa_ref, sb_ref, lam_ref, g_ref, o_ref, qs_ref, ks_ref):
    s_len = q_ref.shape[0]
    c, sa, sb = c_ref[...], sa_ref[...], sb_ref[...]
    first = lax.broadcasted_iota(jnp.int32, (1, LANES), 1) < DIFF_HEAD_DIM
    lp = lam_ref[...]
    lam = (jnp.exp(jnp.sum(lp[0:1] * lp[1:2], axis=-1, keepdims=True))
           - jnp.exp(jnp.sum(lp[2:3] * lp[3:4], axis=-1, keepdims=True)) + lam_init)
    g = g_ref[...]
    for hh in range(DIFF_HEADS_PER_STEP):
        lanes = slice(hh * LANES, (hh + 1) * LANES)
        q = _rope(q_ref[:, lanes].astype(F32), c, sa, sb, DIFF_ROT // 2) * (DIFF_HEAD_DIM ** -0.5 * LOG2E)
        k = _rope(k_ref[:, lanes].astype(F32), c, sa, sb, DIFF_ROT // 2)
        qs_ref[2 * hh] = jnp.where(first, q, 0.0).astype(BF16)
        qs_ref[2 * hh + 1] = jnp.where(first, 0.0, q).astype(BF16)
        ks_ref[hh] = k.astype(BF16)
    for blk in range(s_len // tq):
        r0, length = blk * tq, (blk + 1) * tq
        for hh in range(DIFF_HEADS_PER_STEP):
            lanes = slice(hh * LANES, (hh + 1) * LANES)
            q_rows = jnp.concatenate([qs_ref[2 * hh, r0:r0 + tq, :], qs_ref[2 * hh + 1, r0:r0 + tq, :]], axis=0)
            o = _causal_block(q_rows, ks_ref[hh, 0:length, :], v_ref[0:length, lanes], blk, tq, 2)
            od = o[:tq] - lam * o[tq:]
            o_ref[r0:r0 + tq, lanes] = (_rms(od, g) * (1.0 - lam_init)).astype(BF16)


def _diff_attention(p_mix, rope_d, lam, subln_g, layer, bsz, s_len, tq=512):
    t = p_mix.shape[0]
    lam_init = 0.8 - 0.6 * math.exp(-0.3 * layer)
    tq = min(tq, s_len)
    hp = DIFF_HEADS_PER_STEP
    blk = lambda off: pl.BlockSpec((s_len, hp * LANES), lambda b, h: (b, off // hp + h))
    tab = pl.BlockSpec((None, s_len, LANES), lambda b, h: (b, 0, 0))
    return pl.pallas_call(
        functools.partial(_diff_body, lam_init, tq),
        grid=(bsz, DIFF_HEADS // hp),
        in_specs=[blk(MIX_AQ), blk(MIX_AK), blk(MIX_AV), tab, tab, tab,
                  pl.BlockSpec((None, 4, DIFF_HEAD_DIM), lambda b, h: (layer, 0, 0)),
                  pl.BlockSpec((None, 1, LANES), lambda b, h: (layer, 0, 0))],
        out_specs=pl.BlockSpec((s_len, hp * LANES), lambda b, h: (b, h)),
        out_shape=jax.ShapeDtypeStruct((t, BRANCH_WIDTH), BF16),
        scratch_shapes=[pltpu.VMEM((2 * hp, s_len, LANES), BF16), pltpu.VMEM((hp, s_len, LANES), BF16)],
        compiler_params=_cparams(("parallel", "arbitrary")),
        name="diff_attn",
    )(p_mix, p_mix, p_mix, *rope_d, lam, subln_g)


def _mla_body(tq, cq_ref, ckv_ref, kr_ref, c_ref, sa_ref, sb_ref, qg_ref, kvg_ref, wq_ref, wkv_ref,
              o_ref, qs_ref, ks_ref, vs_ref):
    s_len = cq_ref.shape[0]
    c, sa, sb = c_ref[...], sa_ref[...], sb_ref[...]
    scale = (MLA_NOPE + MLA_ROPE) ** -0.5 * LOG2E
    cqn = _rms(cq_ref[...].astype(F32), qg_ref[...]).astype(BF16)
    ckvn = _rms(ckv_ref[...].astype(F32), kvg_ref[...]).astype(BF16)
    k_rope = _rope(kr_ref[...].astype(F32), c, sa, sb, MLA_ROPE // 2).astype(BF16)
    for hh in range(MLA_HEADS_PER_STEP):
        qc = _dot(cqn, wq_ref[hh])
        qs_ref[hh, :, 0:LANES] = (qc[:, 0:LANES] * scale).astype(BF16)
        qs_ref[hh, :, LANES:] = (_rope(qc[:, LANES:], c, sa, sb, MLA_ROPE // 2) * scale).astype(BF16)
        kv = _dot(ckvn, wkv_ref[hh])
        ks_ref[hh, :, 0:LANES] = kv[:, 0:LANES].astype(BF16)
        ks_ref[hh, :, LANES:] = k_rope
        vs_ref[hh] = kv[:, LANES:].astype(BF16)
    for blk in range(s_len // tq):
        r0, length = blk * tq, (blk + 1) * tq
        for hh in range(MLA_HEADS_PER_STEP):
            o = _causal_block(qs_ref[hh, r0:r0 + tq, :], ks_ref[hh, 0:length, :], vs_ref[hh, 0:length, :], blk, tq, 1)
            o_ref[r0:r0 + tq, hh * LANES:(hh + 1) * LANES] = o.astype(BF16)


def _mla_attention(p_mix, rope_m, q_norm_g, kv_norm_g, w_uq, w_ukv, layer, bsz, s_len, tq=512):
    t = p_mix.shape[0]
    tq = min(tq, s_len)
    hp = MLA_HEADS_PER_STEP
    tab = pl.BlockSpec((None, s_len, LANES), lambda b, h: (b, 0, 0))
    return pl.pallas_call(
        functools.partial(_mla_body, tq),
        grid=(bsz, MLA_HEADS // hp),
        in_specs=[pl.BlockSpec((s_len, MLA_Q_RANK), lambda b, h: (b, MIX_CQ)),
                  pl.BlockSpec((s_len, LANES), lambda b, h: (b, MIX_CKV)),
                  pl.BlockSpec((s_len, LANES), lambda b, h: (b, MIX_CKR)),
                  tab, tab, tab,
                  pl.BlockSpec((None, 1, MLA_Q_RANK), lambda b, h: (layer, 0, 0)),
                  pl.BlockSpec((None, 1, MLA_KV_RANK), lambda b, h: (layer, 0, 0)),
                  pl.BlockSpec((None, hp, MLA_Q_RANK, 2 * LANES), lambda b, h: (layer, h, 0, 0)),
                  pl.BlockSpec((None, hp, MLA_KV_RANK, 2 * LANES), lambda b, h: (layer, h, 0, 0))],
        out_specs=pl.BlockSpec((s_len, hp * LANES), lambda b, h: (b, h)),
        out_shape=jax.ShapeDtypeStruct((t, BRANCH_WIDTH), BF16),
        scratch_shapes=[pltpu.VMEM((hp, s_len, 2 * LANES), BF16), pltpu.VMEM((hp, s_len, 2 * LANES), BF16),
                        pltpu.VMEM((hp, s_len, LANES), BF16)],
        compiler_params=_cparams(("parallel", "arbitrary")),
        name="mla_attn",
    )(p_mix, p_mix, p_mix, *rope_m, q_norm_g, kv_norm_g, w_uq, w_ukv)


def _chunk_cumsum(x, chunk):
    pos = lax.broadcasted_iota(jnp.int32, x.shape, 0) % chunk
    step = 1
    while step < chunk:
        x = x + jnp.where(pos >= step, pltpu.roll(x, step, 0), 0.0)
        step *= 2
    return x


def _gla_body(q_ref, k_ref, v_ref, r_ref, ps_ref, wg_ref, bg_ref, ng_ref, o_ref, b_ref):
    s_len = q_ref.shape[0]
    ck = GLA_CHUNK
    x = _dot(ps_ref[...].astype(BF16), wg_ref[...]) + bg_ref[...]
    g = (jnp.minimum(x, 0.0) - jnp.log(1.0 + jnp.exp(-jnp.abs(x)))) / GLA_TAU
    b_ref[...] = _chunk_cumsum(g, ck)
    tril = (lax.broadcasted_iota(jnp.int32, (ck, ck), 0) >= lax.broadcasted_iota(jnp.int32, (ck, ck), 1))
    ng = ng_ref[...]

    def chunk(ci, st):
        sl = pl.ds(pl.multiple_of(ci * ck, ck), ck)
        bc = b_ref[sl, :]
        bl = b_ref[pl.ds(ci * ck + ck - 1, 1), :]
        kc = k_ref[sl, :].astype(F32)
        qt = (q_ref[sl, :].astype(F32) * (GLA_DK ** -0.5) * jnp.exp(bc)).astype(BF16)
        kt = (kc * jnp.exp(-bc)).astype(BF16)
        kd = (kc * jnp.exp(bl - bc)).astype(BF16)
        vc = v_ref[sl, :]
        att = jnp.where(tril, _dot_nt(qt, kt), 0.0).astype(BF16)
        o = _dot(att, vc) + _dot_nt(qt, st.astype(BF16))
        o_ref[sl, :] = (_rms(o, ng) * _silu(r_ref[sl, :].astype(F32))).astype(BF16)
        return jnp.exp(bl) * st + _dot_tn(vc, kd)

    n_chunks = s_len // ck
    lax.fori_loop(0, n_chunks, chunk, jnp.zeros((GLA_DV, LANES), F32), unroll=math.gcd(n_chunks, GLA_UNROLL))


def _gla(p_mix, p_small, w_gate2, b_gate, norm_g, layer, bsz, s_len):
    t = p_mix.shape[0]
    blk = lambda off: pl.BlockSpec((s_len, LANES), lambda b, h: (b, off + h))
    return pl.pallas_call(
        _gla_body,
        grid=(bsz, GLA_HEADS),
        in_specs=[blk(MIX_BQ), blk(MIX_BK), blk(MIX_BV), blk(MIX_BR),
                  pl.BlockSpec((s_len, LANES), lambda b, h: (b, 0)),
                  pl.BlockSpec((None, None, LANES, LANES), lambda b, h: (layer, h, 0, 0)),
                  pl.BlockSpec((None, None, 1, LANES), lambda b, h: (layer, h, 0, 0)),
                  pl.BlockSpec((None, 1, LANES), lambda b, h: (layer, 0, 0))],
        out_specs=pl.BlockSpec((s_len, LANES), lambda b, h: (b, h)),
        out_shape=jax.ShapeDtypeStruct((t, BRANCH_WIDTH), BF16),
        scratch_shapes=[pltpu.VMEM((s_len, LANES), F32)],
        compiler_params=_cparams(("parallel", "arbitrary")),
        name="gla",
    )(p_mix, p_mix, p_mix, p_mix, p_small, w_gate2, b_gate, norm_g)


def _causal_conv_silu(x, w, b):
    row = lax.broadcasted_iota(jnp.int32, x.shape, 0)
    y = x * w[SSD_CONV - 1:SSD_CONV] + b
    for shift in range(1, SSD_CONV):
        xs = jnp.where(row >= shift, pltpu.roll(x, shift, 0), 0.0)
        y = y + xs * w[SSD_CONV - 1 - shift:SSD_CONV - shift]
    return _silu(y)


def _lane_col(a, idx, rows):
    return jnp.broadcast_to(a[:, idx:idx + 1], (rows, LANES))


def _ssd_body(bc_ref, x_ref, z_ref, ps_ref, cw_bc_ref, cw_x_ref, cb_bc_ref, cb_x_ref, dtb_ref, alog_ref,
              dsk_ref, ng_ref, o_ref, xs_ref, bs_ref, cs_ref, dt_ref, ac_ref):
    s_len = x_ref.shape[0]
    ck = SSD_CHUNK
    n_pairs = SSD_HEADS // 2
    xs_ref[...] = _causal_conv_silu(x_ref[...].astype(F32), cw_x_ref[...], cb_x_ref[...])
    bcv = _causal_conv_silu(bc_ref[...].astype(F32), cw_bc_ref[...], cb_bc_ref[...])
    bs_ref[...] = bcv[:, 0:SSD_GROUPS * SSD_STATE].astype(BF16)
    cs_ref[...] = bcv[:, SSD_GROUPS * SSD_STATE:].astype(BF16)
    dt = _softplus(ps_ref[...] + dtb_ref[...])
    dt_ref[...] = dt
    ac_ref[...] = _chunk_cumsum(dt * (-jnp.exp(alog_ref[...])), ck)

    tril = (lax.broadcasted_iota(jnp.int32, (ck, ck), 0) >= lax.broadcasted_iota(jnp.int32, (ck, ck), 1))
    first = lax.broadcasted_iota(jnp.int32, (1, LANES), 1) < SSD_HEADDIM

    def chunk(ci, states):
        sl = pl.ds(pl.multiple_of(ci * ck, ck), ck)
        new_states = []
        a = ac_ref[sl, :]
        a_t = a.T
        a_last = ac_ref[pl.ds(ci * ck + ck - 1, 1), :]
        dtc = dt_ref[sl, :]
        for grp in range(SSD_GROUPS):
            bm = bs_ref[sl, grp * SSD_STATE:(grp + 1) * SSD_STATE]
            cm = cs_ref[sl, grp * SSD_STATE:(grp + 1) * SSD_STATE]
            cb = _dot_nt(cm, bm)
            ys = []
            for pp in range(n_pairs // SSD_GROUPS):
                pair = grp * (n_pairs // SSD_GROUPS) + pp
                h0 = SMALL_DT + 2 * pair
                a0, a1 = _lane_col(a, h0, ck), _lane_col(a, h0 + 1, ck)
                a_pair = jnp.where(first, a0, a1)
                dt_pair = jnp.where(first, _lane_col(dtc, h0, ck), _lane_col(dtc, h0 + 1, ck))
                al_pair = jnp.where(first, _lane_col(a_last, h0, 1), _lane_col(a_last, h0 + 1, 1))
                xp = xs_ref[sl, pair * LANES:(pair + 1) * LANES]
                xdt = xp * dt_pair
                m0 = (cb * jnp.where(tril, jnp.exp(a0 - a_t[h0:h0 + 1, :]), 0.0)).astype(BF16)
                m1 = (cb * jnp.where(tril, jnp.exp(a1 - a_t[h0 + 1:h0 + 2, :]), 0.0)).astype(BF16)
                y = (_dot(m0, jnp.where(first, xdt, 0.0).astype(BF16))
                     + _dot(m1, jnp.where(first, 0.0, xdt).astype(BF16)))
                st = states[pair]
                y = y + _dot(cm, st.astype(BF16)) * jnp.exp(a_pair)
                new_states.append(jnp.exp(al_pair) * st + _dot_tn(bm, (xdt * jnp.exp(al_pair - a_pair)).astype(BF16)))
                ys.append(y + dsk_ref[:, pair * LANES:(pair + 1) * LANES] * xp)
            width = SSD_INNER // SSD_GROUPS
            lo = grp * width
            yg = jnp.concatenate(ys, axis=1) * _silu(z_ref[sl, lo:lo + width].astype(F32))
            o_ref[sl, lo:lo + width] = _rms(yg, ng_ref[:, lo:lo + width]).astype(BF16)
        return tuple(new_states)

    n_chunks = s_len // ck
    zero = jnp.zeros((SSD_STATE, LANES), F32)
    lax.fori_loop(0, n_chunks, chunk, (zero,) * n_pairs, unroll=math.gcd(n_chunks, 2))


def _ssd(p_mix, p_small, conv_w, conv_b, dt_bias, a_log, d_skip, norm_g, layer, bsz, s_len):
    t = p_mix.shape[0]
    wide = SSD_INNER
    row = lambda width, idx: pl.BlockSpec((s_len, width), lambda b: (b, idx))
    par = lambda rows, width, idx: pl.BlockSpec((None, rows, width), lambda b: (layer, 0, idx))
    return pl.pallas_call(
        _ssd_body,
        grid=(bsz,),
        in_specs=[row(wide, MIX_SBC * LANES // wide), row(wide, MIX_SX * LANES // wide),
                  row(wide, MIX_SZ * LANES // wide), row(LANES, 0),
                  par(SSD_CONV, wide, 1), par(SSD_CONV, wide, 0), par(1, wide, 1), par(1, wide, 0),
                  par(1, LANES, 0), par(1, LANES, 0), par(1, wide, 0), par(1, wide, 0)],
        out_specs=pl.BlockSpec((s_len, wide), lambda b: (b, 0)),
        out_shape=jax.ShapeDtypeStruct((t, wide), BF16),
        scratch_shapes=[pltpu.VMEM((s_len, wide), F32), pltpu.VMEM((s_len, SSD_GROUPS * SSD_STATE), BF16),
                        pltpu.VMEM((s_len, SSD_GROUPS * SSD_STATE), BF16), pltpu.VMEM((s_len, LANES), F32),
                        pltpu.VMEM((s_len, LANES), F32)],
        compiler_params=_cparams(("parallel",)),
        name="ssd",
    )(p_mix, p_mix, p_mix, p_small, conv_w, conv_w, conv_b, conv_b, dt_bias, a_log, d_skip, norm_g)


def _merge_body(alpha, oa_ref, ob_ref, oc_ref, od_ref, gt_ref, h_ref, wb_ref, wo_ref, g_ref, b_ref, o_ref):
    d = h_ref.shape[1]
    merged = None
    for n, br in enumerate((oa_ref, ob_ref, oc_ref, od_ref)):
        y = gt_ref[:, n * d:(n + 1) * d].astype(F32) * _dot(br[...], wb_ref[n])
        merged = y if merged is None else merged + y
    y = alpha * h_ref[...] + _dot(merged.astype(BF16), wo_ref[...])
    o_ref[...] = _layer_norm(y, g_ref[...], b_ref[...])


def _merge(outs, gates, h, w_branch, w_out, ln_g, ln_b, layer, alpha, tm=256):
    t, d = h.shape
    tm = min(tm, t)
    bw = outs[0].shape[1]
    once = pl.Buffered(1)
    return pl.pallas_call(
        functools.partial(_merge_body, alpha),
        grid=(t // tm,),
        in_specs=[pl.BlockSpec((tm, bw), lambda i: (i, 0))] * 4 + [
            pl.BlockSpec((tm, N_BRANCH * d), lambda i: (i, 0)),
            pl.BlockSpec((tm, d), lambda i: (i, 0)),
            pl.BlockSpec((None, N_BRANCH, bw, d), lambda i: (layer, 0, 0, 0), pipeline_mode=once),
            pl.BlockSpec((None, d, d), lambda i: (layer, 0, 0), pipeline_mode=once),
            pl.BlockSpec((None, None, 1, d), lambda i: (layer, 1, 0, 0)),
            pl.BlockSpec((None, None, 1, d), lambda i: (layer, 1, 0, 0)),
        ],
        out_specs=pl.BlockSpec((tm, d), lambda i: (i, 0)),
        out_shape=jax.ShapeDtypeStruct((t, d), F32),
        compiler_params=_cparams(("parallel",)),
        name="merge",
    )(*outs, gates, h, w_branch, w_out, ln_g, ln_b)


def _rope_lane_tables(positions, rot_dim, group):
    inv = ROPE_THETA ** (-jnp.arange(0, rot_dim, 2, dtype=F32) / rot_dim)
    ang = positions.astype(F32)[..., None] * inv
    cos, sin = jnp.cos(ang), jnp.sin(ang)
    half = rot_dim // 2
    rest = group - rot_dim
    shape = cos.shape[:-1]
    ones = jnp.ones(shape + (rest,), F32)
    c = jnp.concatenate([cos, cos, ones], axis=-1)
    sa = jnp.concatenate([-sin, jnp.zeros(shape + (half + rest,), F32)], axis=-1)
    sb = jnp.concatenate([jnp.zeros(shape + (half,), F32), sin, jnp.zeros(shape + (rest,), F32)], axis=-1)
    reps = LANES // group
    return tuple(jnp.tile(t, (1, 1, reps)) for t in (c, sa, sb))


def _pad_last(x, width):
    return jnp.pad(x, [(0, 0)] * (x.ndim - 1) + [(0, width - x.shape[-1])])


def _prepare_in_proj(w_in):
    w_in = w_in.astype(BF16)
    offs = [0]
    for sz in IN_SPLITS:
        offs.append(offs[-1] + sz)
    seg = lambda i: w_in[..., offs[i]:offs[i + 1]]
    (a_q, a_k, a_v, b_q, b_k, b_v, b_glow, b_r, c_q, c_kv, c_kr, d_z, d_xbc, d_dt, gate) = [seg(i) for i in range(15)]
    lead = w_in.shape[:-1]

    def pad_heads(w):
        return _pad_last(w.reshape(lead + (GLA_HEADS, GLA_DK)), LANES).reshape(lead + (GLA_HEADS * LANES,))

    d_x, d_bc = d_xbc[..., :SSD_INNER], d_xbc[..., SSD_INNER:]
    w_mix = jnp.concatenate([c_q, c_kv, a_q, a_k, a_v, pad_heads(b_q), pad_heads(b_k), b_v, b_r,
                             d_bc, d_x, d_z, _pad_last(c_kr, LANES)], axis=-1)
    w_small = _pad_last(jnp.concatenate([b_glow, d_dt], axis=-1), LANES)
    return w_mix, w_small, gate


def kernel(x, positions, ln_g, ln_b, ffn1_w_gu, ffn1_w_down, ffn2_w_gu, ffn2_w_down, w_in, b_gate, diff_lambda, diff_subln_g, gla_w_gate2, gla_b_gate, gla_norm_g, mla_q_norm_g, mla_w_uq, mla_kv_norm_g, mla_w_ukv, ssd_conv_w, ssd_conv_b, ssd_dt_bias, ssd_a_log, ssd_d, ssd_norm_g, w_branch, w_out):
    bsz, s_len, d = x.shape
    depth = ln_g.shape[0]
    alpha = (2 * depth) ** 0.25
    t = bsz * s_len

    rope_d = _rope_lane_tables(positions, DIFF_ROT, DIFF_HEAD_DIM)
    rope_m = _rope_lane_tables(positions, MLA_ROPE, LANES)

    w1_gu, w1_dn = ffn1_w_gu.astype(BF16), ffn1_w_down.astype(BF16)
    w2_gu, w2_dn = ffn2_w_gu.astype(BF16), ffn2_w_down.astype(BF16)
    w_mix, w_small, w_gate = _prepare_in_proj(w_in)
    b_gate3 = b_gate[:, None, :]
    ln_g4, ln_b4 = ln_g[:, :, None, :], ln_b[:, :, None, :]
    subln3 = diff_subln_g[:, None, :]
    wg2 = gla_w_gate2.reshape(depth, GLA_GATE_RANK, GLA_HEADS, GLA_DK).transpose(0, 2, 1, 3)
    wg2 = jnp.pad(wg2, ((0, 0), (0, 0), (SMALL_GLOW, LANES - GLA_GATE_RANK - SMALL_GLOW), (0, LANES - GLA_DK))).astype(BF16)
    bg2 = _pad_last(gla_b_gate.reshape(depth, GLA_HEADS, 1, GLA_DK), LANES)
    gla_ng3 = gla_norm_g[:, None, :]
    wq = mla_w_uq.reshape(depth, MLA_Q_RANK, MLA_HEADS, MLA_NOPE + MLA_ROPE).transpose(0, 2, 1, 3)
    wq = _pad_last(wq, 2 * LANES).astype(BF16)
    wkv = mla_w_ukv.reshape(depth, MLA_KV_RANK, MLA_HEADS, MLA_NOPE + MLA_V).transpose(0, 2, 1, 3).astype(BF16)
    mla_qg3, mla_kvg3 = mla_q_norm_g[:, None, :], mla_kv_norm_g[:, None, :]
    dtb3 = jnp.pad(ssd_dt_bias, ((0, 0), (SMALL_DT, LANES - SMALL_DT - SSD_HEADS)))[:, None, :]
    alog3 = jnp.pad(ssd_a_log, ((0, 0), (SMALL_DT, LANES - SMALL_DT - SSD_HEADS)))[:, None, :]
    dsk3 = jnp.repeat(ssd_d, SSD_HEADDIM, axis=-1)[:, None, :]
    cb3 = ssd_conv_b[:, None, :]
    ssd_ng3 = ssd_norm_g[:, None, :]
    wbr = w_branch.astype(BF16)
    wout = w_out.astype(BF16)

    h = x.reshape(t, d)
    for l in range(depth):
        h = _ffn(h, w1_gu, w1_dn, ln_g4, ln_b4, l, 0, alpha)
        p_mix, p_small, hb = _inproj(h, w_mix, w_small, l)
        gates = _gateproj(hb, w_gate, b_gate3, l)
        o_a = _diff_attention(p_mix, rope_d, diff_lambda, subln3, l, bsz, s_len)
        o_b = _gla(p_mix, p_small, wg2, bg2, gla_ng3, l, bsz, s_len)
        o_c = _mla_attention(p_mix, rope_m, mla_qg3, mla_kvg3, wq, wkv, l, bsz, s_len)
        o_d = _ssd(p_mix, p_small, ssd_conv_w, cb3, dtb3, alog3, dsk3, ssd_ng3, l, bsz, s_len)
        h = _merge((o_a, o_b, o_c, o_d), gates, h, wbr, wout, ln_g4, ln_b4, l, alpha)
        h = _ffn(h, w2_gu, w2_dn, ln_g4, ln_b4, l, 2, alpha)
    return h.reshape(bsz, s_len, d)
```

```python
import functools
import math

import jax
import jax.numpy as jnp
from jax import lax
from jax.experimental import pallas as pl
from jax.experimental.pallas import tpu as pltpu

F32 = jnp.float32
BF16 = jnp.bfloat16

D_MODEL = 2048
N_BRANCH = 4
BRANCH_WIDTH = D_MODEL // 4
ROPE_THETA = 500000.0
NORM_EPS = 1e-5

DIFF_HEADS = 4
DIFF_HEAD_DIM = 64
DIFF_ROT = 16

GLA_HEADS = 4
GLA_DK = 64
GLA_DV = 128
GLA_GATE_RANK = 16
GLA_TAU = 16.0
GLA_CHUNK = 64
GLA_UNROLL = 16

MLA_HEADS = 4
MLA_NOPE = 128
MLA_ROPE = 64
MLA_V = 128
MLA_Q_RANK = 384
MLA_KV_RANK = 128

SSD_HEADDIM = 64
SSD_INNER = 512
SSD_HEADS = 8
SSD_GROUPS = 2
SSD_STATE = 128
SSD_CONV = 4
SSD_CHUNK = 128

D_FF = 5632

IN_SPLITS = [512, 512, 512, 256, 256, 512, 16, 512, 384, 128, 64, 512, 1024, 8, 4 * D_MODEL]

LANES = 128
VMEM_LIMIT_BYTES = 63 * 2**20
FFN_TM = 1024
DIFF_HEADS_PER_STEP = 2
MLA_HEADS_PER_STEP = 2

MIX_CQ = 0
MIX_CKV = 3
MIX_AQ = 4
MIX_AK = 8
MIX_AV = 12
MIX_BQ = 16
MIX_BK = 20
MIX_BV = 24
MIX_BR = 28
MIX_SBC = 32
MIX_SX = 36
MIX_SZ = 40
MIX_CKR = 44
MIX_WIDTH = 45 * LANES
SMALL_GLOW = 0
SMALL_DT = 16

NEG_BIG = -1e30
LOG2E = math.log2(math.e)


def _cparams(semantics):
    return pltpu.CompilerParams(dimension_semantics=semantics, vmem_limit_bytes=VMEM_LIMIT_BYTES)


def _layer_norm(y, g, b):
    mu = jnp.mean(y, axis=-1, keepdims=True)
    yc = y - mu
    var = jnp.mean(yc * yc, axis=-1, keepdims=True)
    return yc * lax.rsqrt(var + NORM_EPS) * g + b


def _rms(x, g):
    return x * lax.rsqrt(jnp.mean(x * x, axis=-1, keepdims=True) + NORM_EPS) * g


def _sigmoid(x):
    return 1.0 / (1.0 + jnp.exp(-x))


def _silu(x):
    return x * _sigmoid(x)


def _softplus(x):
    return jnp.maximum(x, 0.0) + jnp.log(1.0 + jnp.exp(-jnp.abs(x)))


def _dot(a, b):
    return jnp.dot(a, b, preferred_element_type=F32)


def _dot_nt(a, b):
    return lax.dot_general(a, b, (((1,), (1,)), ((), ())), preferred_element_type=F32)


def _dot_tn(a, b):
    return lax.dot_general(a, b, (((0,), (0,)), ((), ())), preferred_element_type=F32)


def _ffn_body(alpha, nj, h_ref, wg_ref, wu_ref, wd_ref, g_ref, b_ref, o_ref, hb_ref):
    j = pl.program_id(1)

    def chunk_product():
        hb = hb_ref[...]
        gate = _dot(hb, wg_ref[...])
        up = _dot(hb, wu_ref[...])
        act = (_silu(gate) * up).astype(BF16)
        return _dot(act, wd_ref[...])

    @pl.when(j == 0)
    def _():
        h = h_ref[...]
        hb_ref[...] = h.astype(BF16)
        o_ref[...] = (2.0 * alpha) * h + chunk_product()

    @pl.when(j > 0)
    def _():
        o_ref[...] += chunk_product()

    @pl.when(j == nj - 1)
    def _():
        o_ref[...] = _layer_norm(0.5 * o_ref[...], g_ref[...], b_ref[...])


def _ffn(h, w_gu, w_down, ln_g, ln_b, layer, ln_idx, alpha, tm=FFN_TM, tf=512):
    t, d = h.shape
    dff = w_down.shape[1]
    nj = dff // tf
    tm = min(tm, t)
    grid = (t // tm, nj)
    return pl.pallas_call(
        functools.partial(_ffn_body, alpha, nj),
        grid=grid,
        in_specs=[
            pl.BlockSpec((tm, d), lambda i, j: (i, 0)),
            pl.BlockSpec((None, d, tf), lambda i, j: (layer, 0, j)),
            pl.BlockSpec((None, d, tf), lambda i, j: (layer, 0, nj + j)),
            pl.BlockSpec((None, tf, d), lambda i, j: (layer, j, 0)),
            pl.BlockSpec((None, None, 1, d), lambda i, j: (layer, ln_idx, 0, 0)),
            pl.BlockSpec((None, None, 1, d), lambda i, j: (layer, ln_idx, 0, 0)),
        ],
        out_specs=pl.BlockSpec((tm, d), lambda i, j: (i, 0)),
        out_shape=jax.ShapeDtypeStruct((t, d), F32),
        scratch_shapes=[pltpu.VMEM((tm, d), BF16)],
        compiler_params=_cparams(("parallel", "arbitrary")),
        name="ffn",
    )(h, w_gu, w_gu, w_down, ln_g, ln_b)


def _inproj_body(h_ref, w_ref, ws_ref, p_ref, ps_ref, hb_ref):
    @pl.when(pl.program_id(1) == 0)
    def _():
        hb_ref[...] = h_ref[...].astype(BF16)
        ps_ref[...] = _dot(hb_ref[...], ws_ref[...])

    p_ref[...] = _dot(hb_ref[...], w_ref[...]).astype(BF16)


def _inproj(h, w_mix, w_small, layer, tm=1024, tn=1920):
    t, d = h.shape
    n = w_mix.shape[2]
    tm = min(tm, t)
    return pl.pallas_call(
        _inproj_body,
        grid=(t // tm, n // tn),
        in_specs=[
            pl.BlockSpec((tm, d), lambda i, j: (i, 0)),
            pl.BlockSpec((None, d, tn), lambda i, j: (layer, 0, j)),
            pl.BlockSpec((None, d, LANES), lambda i, j: (layer, 0, 0)),
        ],
        out_specs=[
            pl.BlockSpec((tm, tn), lambda i, j: (i, j)),
            pl.BlockSpec((tm, LANES), lambda i, j: (i, 0)),
            pl.BlockSpec((tm, d), lambda i, j: (i, 0)),
        ],
        out_shape=[jax.ShapeDtypeStruct((t, n), BF16), jax.ShapeDtypeStruct((t, LANES), F32),
                   jax.ShapeDtypeStruct((t, d), BF16)],
        compiler_params=_cparams(("parallel", "arbitrary")),
        name="inproj",
    )(h, w_mix, w_small)


def _gate_body(hb_ref, w_ref, b_ref, o_ref):
    z = _dot(hb_ref[...], w_ref[...]) + b_ref[...]
    o_ref[...] = _sigmoid(z).astype(BF16)


def _gateproj(hb, w_gate, b_gate, layer, tm=1024, tn=2048):
    t, d = hb.shape
    n = w_gate.shape[2]
    tm = min(tm, t)
    return pl.pallas_call(
        _gate_body,
        grid=(t // tm, n // tn),
        in_specs=[
            pl.BlockSpec((tm, d), lambda i, j: (i, 0)),
            pl.BlockSpec((None, d, tn), lambda i, j: (layer, 0, j)),
            pl.BlockSpec((None, 1, tn), lambda i, j: (layer, 0, j)),
        ],
        out_specs=pl.BlockSpec((tm, tn), lambda i, j: (i, j)),
        out_shape=jax.ShapeDtypeStruct((t, n), BF16),
        compiler_params=_cparams(("parallel", "arbitrary")),
        name="gateproj",
    )(hb, w_gate, b_gate)


def _rope(x, c, sa, sb, half):
    return x * c + pltpu.roll(x, LANES - half, 1) * sa + pltpu.roll(x, half, 1) * sb


def _causal_block(q_rows, k, v, blk, tq, n_maps):
    s = _dot_nt(q_rows, k)
    rows = lax.broadcasted_iota(jnp.int32, s.shape, 0)
    if n_maps > 1:
        rows = jnp.where(rows >= tq, rows - tq, rows)
    cols = lax.broadcasted_iota(jnp.int32, s.shape, 1)
    s = jnp.where(cols <= rows + blk * tq, s, NEG_BIG)
    m = jnp.max(s, axis=-1, keepdims=True)
    p = jnp.exp2(s - m)
    l = jnp.sum(p, axis=-1, keepdims=True)
    return _dot(p.astype(BF16), v) / l


def _diff_body(lam_init, tq, q_ref, k_ref, v_ref, c_ref, sa_ref, sb_ref, lam_ref, g_ref, o_ref, qs_ref, ks_ref):
    s_len = q_ref.shape[0]
    c, sa, sb = c_ref[...], sa_ref[...], sb_ref[...]
    first = lax.broadcasted_iota(jnp.int32, (1, LANES), 1) < DIFF_HEAD_DIM
    lp = lam_ref[...]
    lam = (jnp.exp(jnp.sum(lp[0:1] * lp[1:2], axis=-1, keepdims=True))
           - jnp.exp(jnp.sum(lp[2:3] * lp[3:4], axis=-1, keepdims=True)) + lam_init)
    g = g_ref[...]
    for hh in range(DIFF_HEADS_PER_STEP):
        lanes = slice(hh * LANES, (hh + 1) * LANES)
        q = _rope(q_ref[:, lanes].astype(F32), c, sa, sb, DIFF_ROT // 2) * (DIFF_HEAD_DIM ** -0.5 * LOG2E)
        k = _rope(k_ref[:, lanes].astype(F32), c, sa, sb, DIFF_ROT // 2)
        qs_ref[2 * hh] = jnp.where(first, q, 0.0).astype(BF16)
        qs_ref[2 * hh + 1] = jnp.where(first, 0.0, q).astype(BF16)
        ks_ref[hh] = k.astype(BF16)
    for blk in reversed(range(s_len // tq)):
        r0, length = blk * tq, (blk + 1) * tq
        for hh in range(DIFF_HEADS_PER_STEP):
            lanes = slice(hh * LANES, (hh + 1) * LANES)
            q_rows = jnp.concatenate([qs_ref[2 * hh, r0:r0 + tq, :], qs_ref[2 * hh + 1, r0:r0 + tq, :]], axis=0)
            o = _causal_block(q_rows, ks_ref[hh, 0:length, :], v_ref[0:length, lanes], blk, tq, 2)
            od = o[:tq] - lam * o[tq:]
            o_ref[r0:r0 + tq, lanes] = (_rms(od, g) * (1.0 - lam_init)).astype(BF16)


def _diff_attention(p_mix, rope_d, lam, subln_g, layer, bsz, s_len, tq=256):
    t = p_mix.shape[0]
    lam_init = 0.8 - 0.6 * math.exp(-0.3 * layer)
    tq = min(tq, s_len)
    hp = DIFF_HEADS_PER_STEP
    blk = lambda off: pl.BlockSpec((s_len, hp * LANES), lambda b, h: (b, off // hp + h))
    tab = pl.BlockSpec((None, s_len, LANES), lambda b, h: (b, 0, 0))
    return pl.pallas_call(
        functools.partial(_diff_body, lam_init, tq),
        grid=(bsz, DIFF_HEADS // hp),
        in_specs=[blk(MIX_AQ), blk(MIX_AK), blk(MIX_AV), tab, tab, tab,
                  pl.BlockSpec((None, 4, DIFF_HEAD_DIM), lambda b, h: (layer, 0, 0)),
                  pl.BlockSpec((None, 1, LANES), lambda b, h: (layer, 0, 0))],
        out_specs=pl.BlockSpec((s_len, hp * LANES), lambda b, h: (b, h)),
        out_shape=jax.ShapeDtypeStruct((t, BRANCH_WIDTH), BF16),
        scratch_shapes=[pltpu.VMEM((2 * hp, s_len, LANES), BF16), pltpu.VMEM((hp, s_len, LANES), BF16)],
        compiler_params=_cparams(("parallel", "arbitrary")),
        name="diff_attn",
    )(p_mix, p_mix, p_mix, *rope_d, lam, subln_g)


def _mla_body(tq, cq_ref, ckv_ref, kr_ref, c_ref, sa_ref, sb_ref, qg_ref, kvg_ref, wq_ref, wkv_ref,
              o_ref, qs_ref, ks_ref, vs_ref):
    s_len = cq_ref.shape[0]
    c, sa, sb = c_ref[...], sa_ref[...], sb_ref[...]
    scale = (MLA_NOPE + MLA_ROPE) ** -0.5 * LOG2E
    cqn = _rms(cq_ref[...].astype(F32), qg_ref[...]).astype(BF16)
    ckvn = _rms(ckv_ref[...].astype(F32), kvg_ref[...]).astype(BF16)
    k_rope = _rope(kr_ref[...].astype(F32), c, sa, sb, MLA_ROPE // 2).astype(BF16)
    for hh in range(MLA_HEADS_PER_STEP):
        qc = _dot(cqn, wq_ref[hh])
        qs_ref[hh, :, 0:LANES] = (qc[:, 0:LANES] * scale).astype(BF16)
        qs_ref[hh, :, LANES:] = (_rope(qc[:, LANES:], c, sa, sb, MLA_ROPE // 2) * scale).astype(BF16)
        kv = _dot(ckvn, wkv_ref[hh])
        ks_ref[hh, :, 0:LANES] = kv[:, 0:LANES].astype(BF16)
        ks_ref[hh, :, LANES:] = k_rope
        vs_ref[hh] = kv[:, LANES:].astype(BF16)
    for blk in reversed(range(s_len // tq)):
        r0, length = blk * tq, (blk + 1) * tq
        for hh in range(MLA_HEADS_PER_STEP):
            o = _causal_block(qs_ref[hh, r0:r0 + tq, :], ks_ref[hh, 0:length, :], vs_ref[hh, 0:length, :], blk, tq, 1)
            o_ref[r0:r0 + tq, hh * LANES:(hh + 1) * LANES] = o.astype(BF16)


def _mla_attention(p_mix, rope_m, q_norm_g, kv_norm_g, w_uq, w_ukv, layer, bsz, s_len, tq=512):
    t = p_mix.shape[0]
    tq = min(tq, s_len)
    hp = MLA_HEADS_PER_STEP
    tab = pl.BlockSpec((None, s_len, LANES), lambda b, h: (b, 0, 0))
    return pl.pallas_call(
        functools.partial(_mla_body, tq),
        grid=(bsz, MLA_HEADS // hp),
        in_specs=[pl.BlockSpec((s_len, MLA_Q_RANK), lambda b, h: (b, MIX_CQ)),
                  pl.BlockSpec((s_len, LANES), lambda b, h: (b, MIX_CKV)),
                  pl.BlockSpec((s_len, LANES), lambda b, h: (b, MIX_CKR)),
                  tab, tab, tab,
                  pl.BlockSpec((None, 1, MLA_Q_RANK), lambda b, h: (layer, 0, 0)),
                  pl.BlockSpec((None, 1, MLA_KV_RANK), lambda b, h: (layer, 0, 0)),
                  pl.BlockSpec((None, hp, MLA_Q_RANK, 2 * LANES), lambda b, h: (layer, h, 0, 0)),
                  pl.BlockSpec((None, hp, MLA_KV_RANK, 2 * LANES), lambda b, h: (layer, h, 0, 0))],
        out_specs=pl.BlockSpec((s_len, hp * LANES), lambda b, h: (b, h)),
        out_shape=jax.ShapeDtypeStruct((t, BRANCH_WIDTH), BF16),
        scratch_shapes=[pltpu.VMEM((hp, s_len, 2 * LANES), BF16), pltpu.VMEM((hp, s_len, 2 * LANES), BF16),
                        pltpu.VMEM((hp, s_len, LANES), BF16)],
        compiler_params=_cparams(("parallel", "arbitrary")),
        name="mla_attn",
    )(p_mix, p_mix, p_mix, *rope_m, q_norm_g, kv_norm_g, w_uq, w_ukv)


def _chunk_cumsum(x, chunk):
    pos = lax.broadcasted_iota(jnp.int32, x.shape, 0) % chunk
    step = 1
    while step < chunk:
        x = x + jnp.where(pos >= step, pltpu.roll(x, step, 0), 0.0)
        step *= 2
    return x


def _gla_body(q_ref, k_ref, v_ref, r_ref, ps_ref, wg_ref, bg_ref, ng_ref, o_ref, b_ref):
    s_len = q_ref.shape[0]
    ck = GLA_CHUNK
    x = _dot(ps_ref[...].astype(BF16), wg_ref[...]) + bg_ref[...]
    g = (jnp.minimum(x, 0.0) - jnp.log(1.0 + jnp.exp(-jnp.abs(x)))) / GLA_TAU
    b_ref[...] = _chunk_cumsum(g, ck)
    tril = (lax.broadcasted_iota(jnp.int32, (ck, ck), 0) >= lax.broadcasted_iota(jnp.int32, (ck, ck), 1))
    ng = ng_ref[...]

    def chunk(ci, st):
        sl = pl.ds(pl.multiple_of(ci * ck, ck), ck)
        bc = b_ref[sl, :]
        bl = b_ref[pl.ds(ci * ck + ck - 1, 1), :]
        kc = k_ref[sl, :].astype(F32)
        qt = (q_ref[sl, :].astype(F32) * (GLA_DK ** -0.5) * jnp.exp(bc)).astype(BF16)
        kt = (kc * jnp.exp(-bc)).astype(BF16)
        kd = (kc * jnp.exp(bl - bc)).astype(BF16)
        vc = v_ref[sl, :]
        att = jnp.where(tril, _dot_nt(qt, kt), 0.0).astype(BF16)
        o = _dot(att, vc) + _dot_nt(qt, st.astype(BF16))
        o_ref[sl, :] = (_rms(o, ng) * _silu(r_ref[sl, :].astype(F32))).astype(BF16)
        return jnp.exp(bl) * st + _dot_tn(vc, kd)

    n_chunks = s_len // ck
    lax.fori_loop(0, n_chunks, chunk, jnp.zeros((GLA_DV, LANES), F32), unroll=math.gcd(n_chunks, GLA_UNROLL))


def _gla(p_mix, p_small, w_gate2, b_gate, norm_g, layer, bsz, s_len):
    t = p_mix.shape[0]
    blk = lambda off: pl.BlockSpec((s_len, LANES), lambda b, h: (b, off + h))
    return pl.pallas_call(
        _gla_body,
        grid=(bsz, GLA_HEADS),
        in_specs=[blk(MIX_BQ), blk(MIX_BK), blk(MIX_BV), blk(MIX_BR),
                  pl.BlockSpec((s_len, LANES), lambda b, h: (b, 0)),
                  pl.BlockSpec((None, None, LANES, LANES), lambda b, h: (layer, h, 0, 0)),
                  pl.BlockSpec((None, None, 1, LANES), lambda b, h: (layer, h, 0, 0)),
                  pl.BlockSpec((None, 1, LANES), lambda b, h: (layer, 0, 0))],
        out_specs=pl.BlockSpec((s_len, LANES), lambda b, h: (b, h)),
        out_shape=jax.ShapeDtypeStruct((t, BRANCH_WIDTH), BF16),
        scratch_shapes=[pltpu.VMEM((s_len, LANES), F32)],
        compiler_params=_cparams(("parallel", "arbitrary")),
        name="gla",
    )(p_mix, p_mix, p_mix, p_mix, p_small, w_gate2, b_gate, norm_g)


def _causal_conv_silu(x, w, b):
    row = lax.broadcasted_iota(jnp.int32, x.shape, 0)
    y = x * w[SSD_CONV - 1:SSD_CONV] + b
    for shift in range(1, SSD_CONV):
        xs = jnp.where(row >= shift, pltpu.roll(x, shift, 0), 0.0)
        y = y + xs * w[SSD_CONV - 1 - shift:SSD_CONV - shift]
    return _silu(y)


def _lane_col(a, idx, rows):
    return jnp.broadcast_to(a[:, idx:idx + 1], (rows, LANES))


def _ssd_body(bc_ref, x_ref, z_ref, ps_ref, cw_bc_ref, cw_x_ref, cb_bc_ref, cb_x_ref, dtb_ref, alog_ref,
              dsk_ref, ng_ref, o_ref, xs_ref, bs_ref, cs_ref, dt_ref, ac_ref):
    s_len = x_ref.shape[0]
    ck = SSD_CHUNK
    n_pairs = SSD_HEADS // 2
    xs_ref[...] = _causal_conv_silu(x_ref[...].astype(F32), cw_x_ref[...], cb_x_ref[...])
    bcv = _causal_conv_silu(bc_ref[...].astype(F32), cw_bc_ref[...], cb_bc_ref[...])
    bs_ref[...] = bcv[:, 0:SSD_GROUPS * SSD_STATE].astype(BF16)
    cs_ref[...] = bcv[:, SSD_GROUPS * SSD_STATE:].astype(BF16)
    dt = _softplus(ps_ref[...] + dtb_ref[...])
    dt_ref[...] = dt
    ac_ref[...] = _chunk_cumsum(dt * (-jnp.exp(alog_ref[...])), ck)

    tril = (lax.broadcasted_iota(jnp.int32, (ck, ck), 0) >= lax.broadcasted_iota(jnp.int32, (ck, ck), 1))
    first = lax.broadcasted_iota(jnp.int32, (1, LANES), 1) < SSD_HEADDIM

    def chunk(ci, states):
        sl = pl.ds(pl.multiple_of(ci * ck, ck), ck)
        new_states = []
        a = ac_ref[sl, :]
        a_t = a.T
        a_last = ac_ref[pl.ds(ci * ck + ck - 1, 1), :]
        dtc = dt_ref[sl, :]
        for grp in range(SSD_GROUPS):
            bm = bs_ref[sl, grp * SSD_STATE:(grp + 1) * SSD_STATE]
            cm = cs_ref[sl, grp * SSD_STATE:(grp + 1) * SSD_STATE]
            cb = _dot_nt(cm, bm)
            ys = []
            for pp in range(n_pairs // SSD_GROUPS):
                pair = grp * (n_pairs // SSD_GROUPS) + pp
                h0 = SMALL_DT + 2 * pair
                a0, a1 = _lane_col(a, h0, ck), _lane_col(a, h0 + 1, ck)
                a_pair = jnp.where(first, a0, a1)
                dt_pair = jnp.where(first, _lane_col(dtc, h0, ck), _lane_col(dtc, h0 + 1, ck))
                al_pair = jnp.where(first, _lane_col(a_last, h0, 1), _lane_col(a_last, h0 + 1, 1))
                xp = xs_ref[sl, pair * LANES:(pair + 1) * LANES]
                xdt = xp * dt_pair
                m0 = (cb * jnp.where(tril, jnp.exp(a0 - a_t[h0:h0 + 1, :]), 0.0)).astype(BF16)
                m1 = (cb * jnp.where(tril, jnp.exp(a1 - a_t[h0 + 1:h0 + 2, :]), 0.0)).astype(BF16)
                y = (_dot(m0, jnp.where(first, xdt, 0.0).astype(BF16))
                     + _dot(m1, jnp.where(first, 0.0, xdt).astype(BF16)))
                st = states[pair]
                y = y + _dot(cm, st.astype(BF16)) * jnp.exp(a_pair)
                new_states.append(jnp.exp(al_pair) * st + _dot_tn(bm, (xdt * jnp.exp(al_pair - a_pair)).astype(BF16)))
                ys.append(y + dsk_ref[:, pair * LANES:(pair + 1) * LANES] * xp)
            width = SSD_INNER // SSD_GROUPS
            lo = grp * width
            yg = jnp.concatenate(ys, axis=1) * _silu(z_ref[sl, lo:lo + width].astype(F32))
            o_ref[sl, lo:lo + width] = _rms(yg, ng_ref[:, lo:lo + width]).astype(BF16)
        return tuple(new_states)

    n_chunks = s_len // ck
    zero = jnp.zeros((SSD_STATE, LANES), F32)
    lax.fori_loop(0, n_chunks, chunk, (zero,) * n_pairs, unroll=math.gcd(n_chunks, 2))


def _ssd(p_mix, p_small, conv_w, conv_b, dt_bias, a_log, d_skip, norm_g, layer, bsz, s_len):
    t = p_mix.shape[0]
    wide = SSD_INNER
    row = lambda width, idx: pl.BlockSpec((s_len, width), lambda b: (b, idx))
    par = lambda rows, width, idx: pl.BlockSpec((None, rows, width), lambda b: (layer, 0, idx))
    return pl.pallas_call(
        _ssd_body,
        grid=(bsz,),
        in_specs=[row(wide, MIX_SBC * LANES // wide), row(wide, MIX_SX * LANES // wide),
                  row(wide, MIX_SZ * LANES // wide), row(LANES, 0),
                  par(SSD_CONV, wide, 1), par(SSD_CONV, wide, 0), par(1, wide, 1), par(1, wide, 0),
                  par(1, LANES, 0), par(1, LANES, 0), par(1, wide, 0), par(1, wide, 0)],
        out_specs=pl.BlockSpec((s_len, wide), lambda b: (b, 0)),
        out_shape=jax.ShapeDtypeStruct((t, wide), BF16),
        scratch_shapes=[pltpu.VMEM((s_len, wide), F32), pltpu.VMEM((s_len, SSD_GROUPS * SSD_STATE), BF16),
                        pltpu.VMEM((s_len, SSD_GROUPS * SSD_STATE), BF16), pltpu.VMEM((s_len, LANES), F32),
                        pltpu.VMEM((s_len, LANES), F32)],
        compiler_params=_cparams(("parallel",)),
        name="ssd",
    )(p_mix, p_mix, p_mix, p_small, conv_w, conv_w, conv_b, conv_b, dt_bias, a_log, d_skip, norm_g)


def _merge_body(alpha, oa_ref, ob_ref, oc_ref, od_ref, gt_ref, h_ref, wb_ref, wo_ref, g_ref, b_ref, o_ref):
    d = h_ref.shape[1]
    merged = None
    for n, br in enumerate((oa_ref, ob_ref, oc_ref, od_ref)):
        y = gt_ref[:, n * d:(n + 1) * d].astype(F32) * _dot(br[...], wb_ref[n])
        merged = y if merged is None else merged + y
    y = alpha * h_ref[...] + _dot(merged.astype(BF16), wo_ref[...])
    o_ref[...] = _layer_norm(y, g_ref[...], b_ref[...])


def _merge(outs, gates, h, w_branch, w_out, ln_g, ln_b, layer, alpha, tm=256):
    t, d = h.shape
    tm = min(tm, t)
    bw = outs[0].shape[1]
    once = pl.Buffered(1)
    return pl.pallas_call(
        functools.partial(_merge_body, alpha),
        grid=(t // tm,),
        in_specs=[pl.BlockSpec((tm, bw), lambda i: (i, 0))] * 4 + [
            pl.BlockSpec((tm, N_BRANCH * d), lambda i: (i, 0)),
            pl.BlockSpec((tm, d), lambda i: (i, 0)),
            pl.BlockSpec((None, N_BRANCH, bw, d), lambda i: (layer, 0, 0, 0), pipeline_mode=once),
            pl.BlockSpec((None, d, d), lambda i: (layer, 0, 0), pipeline_mode=once),
            pl.BlockSpec((None, None, 1, d), lambda i: (layer, 1, 0, 0)),
            pl.BlockSpec((None, None, 1, d), lambda i: (layer, 1, 0, 0)),
        ],
        out_specs=pl.BlockSpec((tm, d), lambda i: (i, 0)),
        out_shape=jax.ShapeDtypeStruct((t, d), F32),
        compiler_params=_cparams(("parallel",)),
        name="merge",
    )(*outs, gates, h, w_branch, w_out, ln_g, ln_b)


def _rope_lane_tables(positions, rot_dim, group):
    inv = ROPE_THETA ** (-jnp.arange(0, rot_dim, 2, dtype=F32) / rot_dim)
    ang = positions.astype(F32)[..., None] * inv
    cos, sin = jnp.cos(ang), jnp.sin(ang)
    half = rot_dim // 2
    rest = group - rot_dim
    shape = cos.shape[:-1]
    ones = jnp.ones(shape + (rest,), F32)
    c = jnp.concatenate([cos, cos, ones], axis=-1)
    sa = jnp.concatenate([-sin, jnp.zeros(shape + (half + rest,), F32)], axis=-1)
    sb = jnp.concatenate([jnp.zeros(shape + (half,), F32), sin, jnp.zeros(shape + (rest,), F32)], axis=-1)
    reps = LANES // group
    return tuple(jnp.tile(t, (1, 1, reps)) for t in (c, sa, sb))


def _pad_last(x, width):
    return jnp.pad(x, [(0, 0)] * (x.ndim - 1) + [(0, width - x.shape[-1])])


def _prepare_in_proj(w_in):
    w_in = w_in.astype(BF16)
    offs = [0]
    for sz in IN_SPLITS:
        offs.append(offs[-1] + sz)
    seg = lambda i: w_in[..., offs[i]:offs[i + 1]]
    (a_q, a_k, a_v, b_q, b_k, b_v, b_glow, b_r, c_q, c_kv, c_kr, d_z, d_xbc, d_dt, gate) = [seg(i) for i in range(15)]
    lead = w_in.shape[:-1]

    def pad_heads(w):
        return _pad_last(w.reshape(lead + (GLA_HEADS, GLA_DK)), LANES).reshape(lead + (GLA_HEADS * LANES,))

    d_x, d_bc = d_xbc[..., :SSD_INNER], d_xbc[..., SSD_INNER:]
    w_mix = jnp.concatenate([c_q, c_kv, a_q, a_k, a_v, pad_heads(b_q), pad_heads(b_k), b_v, b_r,
                             d_bc, d_x, d_z, _pad_last(c_kr, LANES)], axis=-1)
    w_small = _pad_last(jnp.concatenate([b_glow, d_dt], axis=-1), LANES)
    return w_mix, w_small, gate


def kernel(x, positions, ln_g, ln_b, ffn1_w_gu, ffn1_w_down, ffn2_w_gu, ffn2_w_down, w_in, b_gate, diff_lambda, diff_subln_g, gla_w_gate2, gla_b_gate, gla_norm_g, mla_q_norm_g, mla_w_uq, mla_kv_norm_g, mla_w_ukv, ssd_conv_w, ssd_conv_b, ssd_dt_bias, ssd_a_log, ssd_d, ssd_norm_g, w_branch, w_out):
    bsz, s_len, d = x.shape
    depth = ln_g.shape[0]
    alpha = (2 * depth) ** 0.25
    t = bsz * s_len

    rope_d = _rope_lane_tables(positions, DIFF_ROT, DIFF_HEAD_DIM)
    rope_m = _rope_lane_tables(positions, MLA_ROPE, LANES)

    w1_gu, w1_dn = ffn1_w_gu.astype(BF16), ffn1_w_down.astype(BF16)
    w2_gu, w2_dn = ffn2_w_gu.astype(BF16), ffn2_w_down.astype(BF16)
    w_mix, w_small, w_gate = _prepare_in_proj(w_in)
    b_gate3 = b_gate[:, None, :]
    ln_g4, ln_b4 = ln_g[:, :, None, :], ln_b[:, :, None, :]
    subln3 = diff_subln_g[:, None, :]
    wg2 = gla_w_gate2.reshape(depth, GLA_GATE_RANK, GLA_HEADS, GLA_DK).transpose(0, 2, 1, 3)
    wg2 = jnp.pad(wg2, ((0, 0), (0, 0), (SMALL_GLOW, LANES - GLA_GATE_RANK - SMALL_GLOW), (0, LANES - GLA_DK))).astype(BF16)
    bg2 = _pad_last(gla_b_gate.reshape(depth, GLA_HEADS, 1, GLA_DK), LANES)
    gla_ng3 = gla_norm_g[:, None, :]
    wq = mla_w_uq.reshape(depth, MLA_Q_RANK, MLA_HEADS, MLA_NOPE + MLA_ROPE).transpose(0, 2, 1, 3)
    wq = _pad_last(wq, 2 * LANES).astype(BF16)
    wkv = mla_w_ukv.reshape(depth, MLA_KV_RANK, MLA_HEADS, MLA_NOPE + MLA_V).transpose(0, 2, 1, 3).astype(BF16)
    mla_qg3, mla_kvg3 = mla_q_norm_g[:, None, :], mla_kv_norm_g[:, None, :]
    dtb3 = jnp.pad(ssd_dt_bias, ((0, 0), (SMALL_DT, LANES - SMALL_DT - SSD_HEADS)))[:, None, :]
    alog3 = jnp.pad(ssd_a_log, ((0, 0), (SMALL_DT, LANES - SMALL_DT - SSD_HEADS)))[:, None, :]
    dsk3 = jnp.repeat(ssd_d, SSD_HEADDIM, axis=-1)[:, None, :]
    cb3 = ssd_conv_b[:, None, :]
    ssd_ng3 = ssd_norm_g[:, None, :]
    wbr = w_branch.astype(BF16)
    wout = w_out.astype(BF16)

    h = x.reshape(t, d)
    for l in range(depth):
        h = _ffn(h, w1_gu, w1_dn, ln_g4, ln_b4, l, 0, alpha)
        p_mix, p_small, hb = _inproj(h, w_mix, w_small, l)
        gates = _gateproj(hb, w_gate, b_gate3, l)
        o_a = _diff_attention(p_mix, rope_d, diff_lambda, subln3, l, bsz, s_len)
        o_b = _gla(p_mix, p_small, wg2, bg2, gla_ng3, l, bsz, s_len)
        o_c = _mla_attention(p_mix, rope_m, mla_qg3, mla_kvg3, wq, wkv, l, bsz, s_len)
        o_d = _ssd(p_mix, p_small, ssd_conv_w, cb3, dtb3, alog3, dsk3, ssd_ng3, l, bsz, s_len)
        h = _merge((o_a, o_b, o_c, o_d), gates, h, wbr, wout, ln_g4, ln_b4, l, alpha)
        h = _ffn(h, w2_gu, w2_dn, ln_g4, ln_b4, l, 2, alpha)
    return h.reshape(bsz, s_len, d)
```

```python
import functools
import math

import jax
import jax.numpy as jnp
from jax import lax
from jax.experimental import pallas as pl
from jax.experimental.pallas import tpu as pltpu

F32 = jnp.float32
BF16 = jnp.bfloat16

D_MODEL = 2048
N_BRANCH = 4
BRANCH_WIDTH = D_MODEL // 4
ROPE_THETA = 500000.0
NORM_EPS = 1e-5

DIFF_HEADS = 4
DIFF_HEAD_DIM = 64
DIFF_ROT = 16

GLA_HEADS = 4
GLA_DK = 64
GLA_DV = 128
GLA_GATE_RANK = 16
GLA_TAU = 16.0
GLA_CHUNK = 64
GLA_UNROLL = 16

MLA_HEADS = 4
MLA_NOPE = 128
MLA_ROPE = 64
MLA_V = 128
MLA_Q_RANK = 384
MLA_KV_RANK = 128

SSD_HEADDIM = 64
SSD_INNER = 512
SSD_HEADS = 8
SSD_GROUPS = 2
SSD_STATE = 128
SSD_CONV = 4
SSD_CHUNK = 128

D_FF = 5632

IN_SPLITS = [512, 512, 512, 256, 256, 512, 16, 512, 384, 128, 64, 512, 1024, 8, 4 * D_MODEL]

LANES = 128
VMEM_LIMIT_BYTES = 63 * 2**20
FFN_TM = 1024
DIFF_HEADS_PER_STEP = 2
MLA_HEADS_PER_STEP = 2

MIX_CQ = 0
MIX_CKV = 3
MIX_AQ = 4
MIX_AK = 8
MIX_AV = 12
MIX_BQ = 16
MIX_BK = 20
MIX_BV = 24
MIX_BR = 28
MIX_SBC = 32
MIX_SX = 36
MIX_SZ = 40
MIX_CKR = 44
MIX_WIDTH = 45 * LANES
SMALL_GLOW = 0
SMALL_DT = 16

NEG_BIG = -1e30
LOG2E = math.log2(math.e)


def _cparams(semantics):
    return pltpu.CompilerParams(dimension_semantics=semantics, vmem_limit_bytes=VMEM_LIMIT_BYTES)


def _layer_norm(y, g, b):
    mu = jnp.mean(y, axis=-1, keepdims=True)
    yc = y - mu
    var = jnp.mean(yc * yc, axis=-1, keepdims=True)
    return yc * lax.rsqrt(var + NORM_EPS) * g + b


def _rms(x, g):
    return x * lax.rsqrt(jnp.mean(x * x, axis=-1, keepdims=True) + NORM_EPS) * g


def _sigmoid(x):
    return 1.0 / (1.0 + jnp.exp(-x))


def _silu(x):
    return x * _sigmoid(x)


def _softplus(x):
    return jnp.maximum(x, 0.0) + jnp.log(1.0 + jnp.exp(-jnp.abs(x)))


def _dot(a, b):
    return jnp.dot(a, b, preferred_element_type=F32)


def _dot_nt(a, b):
    return lax.dot_general(a, b, (((1,), (1,)), ((), ())), preferred_element_type=F32)


def _dot_tn(a, b):
    return lax.dot_general(a, b, (((0,), (0,)), ((), ())), preferred_element_type=F32)


def _ffn_body(alpha, nj, h_ref, wg_ref, wu_ref, wd_ref, g_ref, b_ref, o_ref, hb_ref):
    j = pl.program_id(1)

    def chunk_product():
        hb = hb_ref[...]
        gate = _dot(hb, wg_ref[...])
        up = _dot(hb, wu_ref[...])
        act = (_silu(gate) * up).astype(BF16)
        return _dot(act, wd_ref[...])

    @pl.when(j == 0)
    def _():
        h = h_ref[...]
        hb_ref[...] = h.astype(BF16)
        o_ref[...] = (2.0 * alpha) * h + chunk_product()

    @pl.when(j > 0)
    def _():
        o_ref[...] += chunk_product()

    @pl.when(j == nj - 1)
    def _():
        o_ref[...] = _layer_norm(0.5 * o_ref[...], g_ref[...], b_ref[...])


def _ffn(h, w_gu, w_down, ln_g, ln_b, layer, ln_idx, alpha, tm=FFN_TM, tf=512):
    t, d = h.shape
    dff = w_down.shape[1]
    nj = dff // tf
    tm = min(tm, t)
    grid = (t // tm, nj)
    return pl.pallas_call(
        functools.partial(_ffn_body, alpha, nj),
        grid=grid,
        in_specs=[
            pl.BlockSpec((tm, d), lambda i, j: (i, 0)),
            pl.BlockSpec((None, d, tf), lambda i, j: (layer, 0, j)),
            pl.BlockSpec((None, d, tf), lambda i, j: (layer, 0, nj + j)),
            pl.BlockSpec((None, tf, d), lambda i, j: (layer, j, 0)),
            pl.BlockSpec((None, None, 1, d), lambda i, j: (layer, ln_idx, 0, 0)),
            pl.BlockSpec((None, None, 1, d), lambda i, j: (layer, ln_idx, 0, 0)),
        ],
        out_specs=pl.BlockSpec((tm, d), lambda i, j: (i, 0)),
        out_shape=jax.ShapeDtypeStruct((t, d), F32),
        scratch_shapes=[pltpu.VMEM((tm, d), BF16)],
        compiler_params=_cparams(("parallel", "arbitrary")),
        name="ffn",
    )(h, w_gu, w_gu, w_down, ln_g, ln_b)


def _inproj_body(h_ref, w_ref, ws_ref, p_ref, ps_ref, hb_ref):
    n = pl.program_id(1)

    @pl.when(n == 0)
    def _():
        hb = h_ref[...].astype(BF16)
        hb_ref[...] = hb
        ps_ref[...] = _dot(hb, ws_ref[...])
        p_ref[...] = _dot(hb, w_ref[...]).astype(BF16)

    @pl.when(n > 0)
    def _():
        p_ref[...] = _dot(hb_ref[...], w_ref[...]).astype(BF16)


def _inproj(h, w_mix, w_small, layer, tm=1024, tn=1920):
    t, d = h.shape
    n = w_mix.shape[2]
    tm = min(tm, t)
    return pl.pallas_call(
        _inproj_body,
        grid=(t // tm, n // tn),
        in_specs=[
            pl.BlockSpec((tm, d), lambda i, j: (i, 0)),
            pl.BlockSpec((None, d, tn), lambda i, j: (layer, 0, j)),
            pl.BlockSpec((None, d, LANES), lambda i, j: (layer, 0, 0)),
        ],
        out_specs=[
            pl.BlockSpec((tm, tn), lambda i, j: (i, j)),
            pl.BlockSpec((tm, LANES), lambda i, j: (i, 0)),
            pl.BlockSpec((tm, d), lambda i, j: (i, 0)),
        ],
        out_shape=[jax.ShapeDtypeStruct((t, n), BF16), jax.ShapeDtypeStruct((t, LANES), F32),
                   jax.ShapeDtypeStruct((t, d), BF16)],
        compiler_params=_cparams(("parallel", "arbitrary")),
        name="inproj",
    )(h, w_mix, w_small)


def _gate_body(hb_ref, w_ref, b_ref, o_ref):
    z = _dot(hb_ref[...], w_ref[...]) + b_ref[...]
    o_ref[...] = _sigmoid(z).astype(BF16)


def _gateproj(hb, w_gate, b_gate, layer, tm=1024, tn=2048):
    t, d = hb.shape
    n = w_gate.shape[2]
    tm = min(tm, t)
    return pl.pallas_call(
        _gate_body,
        grid=(t // tm, n // tn),
        in_specs=[
            pl.BlockSpec((tm, d), lambda i, j: (i, 0)),
            pl.BlockSpec((None, d, tn), lambda i, j: (layer, 0, j)),
            pl.BlockSpec((None, 1, tn), lambda i, j: (layer, 0, j)),
        ],
        out_specs=pl.BlockSpec((tm, tn), lambda i, j: (i, j)),
        out_shape=jax.ShapeDtypeStruct((t, n), BF16),
        compiler_params=_cparams(("parallel", "arbitrary")),
        name="gateproj",
    )(hb, w_gate, b_gate)


def _rope(x, c, sa, sb, half):
    return x * c + pltpu.roll(x, LANES - half, 1) * sa + pltpu.roll(x, half, 1) * sb


def _causal_block(q_rows, k, v, blk, tq, n_maps):
    s = _dot_nt(q_rows, k)
    rows = lax.broadcasted_iota(jnp.int32, s.shape, 0)
    if n_maps > 1:
        rows = jnp.where(rows >= tq, rows - tq, rows)
    cols = lax.broadcasted_iota(jnp.int32, s.shape, 1)
    s = jnp.where(cols <= rows + blk * tq, s, NEG_BIG)
    m = jnp.max(s, axis=-1, keepdims=True)
    p = jnp.exp2(s - m)
    l = jnp.sum(p, axis=-1, keepdims=True)
    return _dot(p.astype(BF16), v) / l


def _diff_body(lam_init, tq, q_ref, k_ref, v_ref, c_ref, sa_ref, sb_ref, lam_ref, g_ref, o_ref, qs_ref, ks_ref):
    s_len = q_ref.shape[0]
    c, sa, sb = c_ref[...], sa_ref[...], sb_ref[...]
    first = lax.broadcasted_iota(jnp.int32, (1, LANES), 1) < DIFF_HEAD_DIM
    lp = lam_ref[...]
    lam = (jnp.exp(jnp.sum(lp[0:1] * lp[1:2], axis=-1, keepdims=True))
           - jnp.exp(jnp.sum(lp[2:3] * lp[3:4], axis=-1, keepdims=True)) + lam_init)
    g = g_ref[...]
    for hh in range(DIFF_HEADS_PER_STEP):
        lanes = slice(hh * LANES, (hh + 1) * LANES)
        q = _rope(q_ref[:, lanes].astype(F32), c, sa, sb, DIFF_ROT // 2) * (DIFF_HEAD_DIM ** -0.5 * LOG2E)
        k = _rope(k_ref[:, lanes].astype(F32), c, sa, sb, DIFF_ROT // 2)
        qs_ref[2 * hh] = jnp.where(first, q, 0.0).astype(BF16)
        qs_ref[2 * hh + 1] = jnp.where(first, 0.0, q).astype(BF16)
        ks_ref[hh] = k.astype(BF16)
    for blk in reversed(range(s_len // tq)):
        r0, length = blk * tq, (blk + 1) * tq
        for hh in range(DIFF_HEADS_PER_STEP):
            lanes = slice(hh * LANES, (hh + 1) * LANES)
            q_rows = jnp.concatenate([qs_ref[2 * hh, r0:r0 + tq, :], qs_ref[2 * hh + 1, r0:r0 + tq, :]], axis=0)
            o = _causal_block(q_rows, ks_ref[hh, 0:length, :], v_ref[0:length, lanes], blk, tq, 2)
            od = o[:tq] - lam * o[tq:]
            o_ref[r0:r0 + tq, lanes] = (_rms(od, g) * (1.0 - lam_init)).astype(BF16)


def _diff_attention(p_mix, rope_d, lam, subln_g, layer, bsz, s_len, tq=256):
    t = p_mix.shape[0]
    lam_init = 0.8 - 0.6 * math.exp(-0.3 * layer)
    tq = min(tq, s_len)
    hp = DIFF_HEADS_PER_STEP
    blk = lambda off: pl.BlockSpec((s_len, hp * LANES), lambda b, h: (b, off // hp + h))
    tab = pl.BlockSpec((None, s_len, LANES), lambda b, h: (b, 0, 0))
    return pl.pallas_call(
        functools.partial(_diff_body, lam_init, tq),
        grid=(bsz, DIFF_HEADS // hp),
        in_specs=[blk(MIX_AQ), blk(MIX_AK), blk(MIX_AV), tab, tab, tab,
                  pl.BlockSpec((None, 4, DIFF_HEAD_DIM), lambda b, h: (layer, 0, 0)),
                  pl.BlockSpec((None, 1, LANES), lambda b, h: (layer, 0, 0))],
        out_specs=pl.BlockSpec((s_len, hp * LANES), lambda b, h: (b, h)),
        out_shape=jax.ShapeDtypeStruct((t, BRANCH_WIDTH), BF16),
        scratch_shapes=[pltpu.VMEM((2 * hp, s_len, LANES), BF16), pltpu.VMEM((hp, s_len, LANES), BF16)],
        compiler_params=_cparams(("parallel", "arbitrary")),
        name="diff_attn",
    )(p_mix, p_mix, p_mix, *rope_d, lam, subln_g)


def _mla_body(tq, cq_ref, ckv_ref, kr_ref, c_ref, sa_ref, sb_ref, qg_ref, kvg_ref, wq_ref, wkv_ref,
              o_ref, qs_ref, ks_ref, vs_ref):
    s_len = cq_ref.shape[0]
    c, sa, sb = c_ref[...], sa_ref[...], sb_ref[...]
    scale = (MLA_NOPE + MLA_ROPE) ** -0.5 * LOG2E
    cqn = _rms(cq_ref[...].astype(F32), qg_ref[...]).astype(BF16)
    ckvn = _rms(ckv_ref[...].astype(F32), kvg_ref[...]).astype(BF16)
    k_rope = _rope(kr_ref[...].astype(F32), c, sa, sb, MLA_ROPE // 2).astype(BF16)
    for hh in range(MLA_HEADS_PER_STEP):
        qc = _dot(cqn, wq_ref[hh])
        qs_ref[hh, :, 0:LANES] = (qc[:, 0:LANES] * scale).astype(BF16)
        qs_ref[hh, :, LANES:] = (_rope(qc[:, LANES:], c, sa, sb, MLA_ROPE // 2) * scale).astype(BF16)
        kv = _dot(ckvn, wkv_ref[hh])
        ks_ref[hh, :, 0:LANES] = kv[:, 0:LANES].astype(BF16)
        ks_ref[hh, :, LANES:] = k_rope
        vs_ref[hh] = kv[:, LANES:].astype(BF16)
    for blk in reversed(range(s_len // tq)):
        r0, length = blk * tq, (blk + 1) * tq
        for hh in range(MLA_HEADS_PER_STEP):
            o = _causal_block(qs_ref[hh, r0:r0 + tq, :], ks_ref[hh, 0:length, :], vs_ref[hh, 0:length, :], blk, tq, 1)
            o_ref[r0:r0 + tq, hh * LANES:(hh + 1) * LANES] = o.astype(BF16)


def _mla_attention(p_mix, rope_m, q_norm_g, kv_norm_g, w_uq, w_ukv, layer, bsz, s_len, tq=512):
    t = p_mix.shape[0]
    tq = min(tq, s_len)
    hp = MLA_HEADS_PER_STEP
    tab = pl.BlockSpec((None, s_len, LANES), lambda b, h: (b, 0, 0))
    return pl.pallas_call(
        functools.partial(_mla_body, tq),
        grid=(bsz, MLA_HEADS // hp),
        in_specs=[pl.BlockSpec((s_len, MLA_Q_RANK), lambda b, h: (b, MIX_CQ)),
                  pl.BlockSpec((s_len, LANES), lambda b, h: (b, MIX_CKV)),
                  pl.BlockSpec((s_len, LANES), lambda b, h: (b, MIX_CKR)),
                  tab, tab, tab,
                  pl.BlockSpec((None, 1, MLA_Q_RANK), lambda b, h: (layer, 0, 0)),
                  pl.BlockSpec((None, 1, MLA_KV_RANK), lambda b, h: (layer, 0, 0)),
                  pl.BlockSpec((None, hp, MLA_Q_RANK, 2 * LANES), lambda b, h: (layer, h, 0, 0)),
                  pl.BlockSpec((None, hp, MLA_KV_RANK, 2 * LANES), lambda b, h: (layer, h, 0, 0))],
        out_specs=pl.BlockSpec((s_len, hp * LANES), lambda b, h: (b, h)),
        out_shape=jax.ShapeDtypeStruct((t, BRANCH_WIDTH), BF16),
        scratch_shapes=[pltpu.VMEM((hp, s_len, 2 * LANES), BF16), pltpu.VMEM((hp, s_len, 2 * LANES), BF16),
                        pltpu.VMEM((hp, s_len, LANES), BF16)],
        compiler_params=_cparams(("parallel", "arbitrary")),
        name="mla_attn",
    )(p_mix, p_mix, p_mix, *rope_m, q_norm_g, kv_norm_g, w_uq, w_ukv)


def _chunk_cumsum(x, chunk):
    pos = lax.broadcasted_iota(jnp.int32, x.shape, 0) % chunk
    step = 1
    while step < chunk:
        x = x + jnp.where(pos >= step, pltpu.roll(x, step, 0), 0.0)
        step *= 2
    return x


def _gla_body(q_ref, k_ref, v_ref, r_ref, ps_ref, wg_ref, bg_ref, ng_ref, o_ref, b_ref):
    s_len = q_ref.shape[0]
    ck = GLA_CHUNK
    x = _dot(ps_ref[...].astype(BF16), wg_ref[...]) + bg_ref[...]
    g = (jnp.minimum(x, 0.0) - jnp.log(1.0 + jnp.exp(-jnp.abs(x)))) / GLA_TAU
    b_ref[...] = _chunk_cumsum(g, ck)
    tril = (lax.broadcasted_iota(jnp.int32, (ck, ck), 0) >= lax.broadcasted_iota(jnp.int32, (ck, ck), 1))
    ng = ng_ref[...]

    def chunk(ci, st):
        sl = pl.ds(pl.multiple_of(ci * ck, ck), ck)
        bc = b_ref[sl, :]
        bl = b_ref[pl.ds(ci * ck + ck - 1, 1), :]
        kc = k_ref[sl, :].astype(F32)
        qt = (q_ref[sl, :].astype(F32) * (GLA_DK ** -0.5) * jnp.exp(bc)).astype(BF16)
        kt = (kc * jnp.exp(-bc)).astype(BF16)
        kd = (kc * jnp.exp(bl - bc)).astype(BF16)
        vc = v_ref[sl, :]
        att = jnp.where(tril, _dot_nt(qt, kt), 0.0).astype(BF16)
        o = _dot(att, vc) + _dot_nt(qt, st.astype(BF16))
        o_ref[sl, :] = (_rms(o, ng) * _silu(r_ref[sl, :].astype(F32))).astype(BF16)
        return jnp.exp(bl) * st + _dot_tn(vc, kd)

    n_chunks = s_len // ck
    lax.fori_loop(0, n_chunks, chunk, jnp.zeros((GLA_DV, LANES), F32), unroll=math.gcd(n_chunks, GLA_UNROLL))


def _gla(p_mix, p_small, w_gate2, b_gate, norm_g, layer, bsz, s_len):
    t = p_mix.shape[0]
    blk = lambda off: pl.BlockSpec((s_len, LANES), lambda b, h: (b, off + h))
    return pl.pallas_call(
        _gla_body,
        grid=(bsz, GLA_HEADS),
        in_specs=[blk(MIX_BQ), blk(MIX_BK), blk(MIX_BV), blk(MIX_BR),
                  pl.BlockSpec((s_len, LANES), lambda b, h: (b, 0)),
                  pl.BlockSpec((None, None, LANES, LANES), lambda b, h: (layer, h, 0, 0)),
                  pl.BlockSpec((None, None, 1, LANES), lambda b, h: (layer, h, 0, 0)),
                  pl.BlockSpec((None, 1, LANES), lambda b, h: (layer, 0, 0))],
        out_specs=pl.BlockSpec((s_len, LANES), lambda b, h: (b, h)),
        out_shape=jax.ShapeDtypeStruct((t, BRANCH_WIDTH), BF16),
        scratch_shapes=[pltpu.VMEM((s_len, LANES), F32)],
        compiler_params=_cparams(("parallel", "arbitrary")),
        name="gla",
    )(p_mix, p_mix, p_mix, p_mix, p_small, w_gate2, b_gate, norm_g)


def _causal_conv_silu(x, w, b):
    row = lax.broadcasted_iota(jnp.int32, x.shape, 0)
    y = x * w[SSD_CONV - 1:SSD_CONV] + b
    for shift in range(1, SSD_CONV):
        xs = jnp.where(row >= shift, pltpu.roll(x, shift, 0), 0.0)
        y = y + xs * w[SSD_CONV - 1 - shift:SSD_CONV - shift]
    return _silu(y)


def _lane_col(a, idx, rows):
    return jnp.broadcast_to(a[:, idx:idx + 1], (rows, LANES))


def _ssd_body(bc_ref, x_ref, z_ref, ps_ref, cw_bc_ref, cw_x_ref, cb_bc_ref, cb_x_ref, dtb_ref, alog_ref,
              dsk_ref, ng_ref, o_ref, xs_ref, bs_ref, cs_ref, dt_ref, ac_ref):
    s_len = x_ref.shape[0]
    ck = SSD_CHUNK
    n_pairs = SSD_HEADS // 2
    xs_ref[...] = _causal_conv_silu(x_ref[...].astype(F32), cw_x_ref[...], cb_x_ref[...])
    bcv = _causal_conv_silu(bc_ref[...].astype(F32), cw_bc_ref[...], cb_bc_ref[...])
    bs_ref[...] = bcv[:, 0:SSD_GROUPS * SSD_STATE].astype(BF16)
    cs_ref[...] = bcv[:, SSD_GROUPS * SSD_STATE:].astype(BF16)
    dt = _softplus(ps_ref[...] + dtb_ref[...])
    dt_ref[...] = dt
    ac_ref[...] = _chunk_cumsum(dt * (-jnp.exp(alog_ref[...])), ck)

    tril = (lax.broadcasted_iota(jnp.int32, (ck, ck), 0) >= lax.broadcasted_iota(jnp.int32, (ck, ck), 1))
    first = lax.broadcasted_iota(jnp.int32, (1, LANES), 1) < SSD_HEADDIM

    def chunk(ci, states):
        sl = pl.ds(pl.multiple_of(ci * ck, ck), ck)
        new_states = []
        a = ac_ref[sl, :]
        a_t = a.T
        a_last = ac_ref[pl.ds(ci * ck + ck - 1, 1), :]
        dtc = dt_ref[sl, :]
        for grp in range(SSD_GROUPS):
            bm = bs_ref[sl, grp * SSD_STATE:(grp + 1) * SSD_STATE]
            cm = cs_ref[sl, grp * SSD_STATE:(grp + 1) * SSD_STATE]
            cb = _dot_nt(cm, bm)
            ys = []
            for pp in range(n_pairs // SSD_GROUPS):
                pair = grp * (n_pairs // SSD_GROUPS) + pp
                h0 = SMALL_DT + 2 * pair
                a0, a1 = _lane_col(a, h0, ck), _lane_col(a, h0 + 1, ck)
                a_pair = jnp.where(first, a0, a1)
                dt_pair = jnp.where(first, _lane_col(dtc, h0, ck), _lane_col(dtc, h0 + 1, ck))
                al_pair = jnp.where(first, _lane_col(a_last, h0, 1), _lane_col(a_last, h0 + 1, 1))
                xp = xs_ref[sl, pair * LANES:(pair + 1) * LANES]
                xdt = xp * dt_pair
                m0 = (cb * jnp.where(tril, jnp.exp(a0 - a_t[h0:h0 + 1, :]), 0.0)).astype(BF16)
                m1 = (cb * jnp.where(tril, jnp.exp(a1 - a_t[h0 + 1:h0 + 2, :]), 0.0)).astype(BF16)
                y = (_dot(m0, jnp.where(first, xdt, 0.0).astype(BF16))
                     + _dot(m1, jnp.where(first, 0.0, xdt).astype(BF16)))
                st = states[pair]
                y = y + _dot(cm, st.astype(BF16)) * jnp.exp(a_pair)
                new_states.append(jnp.exp(al_pair) * st + _dot_tn(bm, (xdt * jnp.exp(al_pair - a_pair)).astype(BF16)))
                ys.append(y + dsk_ref[:, pair * LANES:(pair + 1) * LANES] * xp)
            width = SSD_INNER // SSD_GROUPS
            lo = grp * width
            yg = jnp.concatenate(ys, axis=1) * _silu(z_ref[sl, lo:lo + width].astype(F32))
            o_ref[sl, lo:lo + width] = _rms(yg, ng_ref[:, lo:lo + width]).astype(BF16)
        return tuple(new_states)

    n_chunks = s_len // ck
    zero = jnp.zeros((SSD_STATE, LANES), F32)
    lax.fori_loop(0, n_chunks, chunk, (zero,) * n_pairs, unroll=math.gcd(n_chunks, 2))


def _ssd(p_mix, p_small, conv_w, conv_b, dt_bias, a_log, d_skip, norm_g, layer, bsz, s_len):
    t = p_mix.shape[0]
    wide = SSD_INNER
    row = lambda width, idx: pl.BlockSpec((s_len, width), lambda b: (b, idx))
    par = lambda rows, width, idx: pl.BlockSpec((None, rows, width), lambda b: (layer, 0, idx))
    return pl.pallas_call(
        _ssd_body,
        grid=(bsz,),
        in_specs=[row(wide, MIX_SBC * LANES // wide), row(wide, MIX_SX * LANES // wide),
                  row(wide, MIX_SZ * LANES // wide), row(LANES, 0),
                  par(SSD_CONV, wide, 1), par(SSD_CONV, wide, 0), par(1, wide, 1), par(1, wide, 0),
                  par(1, LANES, 0), par(1, LANES, 0), par(1, wide, 0), par(1, wide, 0)],
        out_specs=pl.BlockSpec((s_len, wide), lambda b: (b, 0)),
        out_shape=jax.ShapeDtypeStruct((t, wide), BF16),
        scratch_shapes=[pltpu.VMEM((s_len, wide), F32), pltpu.VMEM((s_len, SSD_GROUPS * SSD_STATE), BF16),
                        pltpu.VMEM((s_len, SSD_GROUPS * SSD_STATE), BF16), pltpu.VMEM((s_len, LANES), F32),
                        pltpu.VMEM((s_len, LANES), F32)],
        compiler_params=_cparams(("parallel",)),
        name="ssd",
    )(p_mix, p_mix, p_mix, p_small, conv_w, conv_w, conv_b, conv_b, dt_bias, a_log, d_skip, norm_g)


def _merge_body(alpha, oa_ref, ob_ref, oc_ref, od_ref, gt_ref, h_ref, wb_ref, wo_ref, g_ref, b_ref, o_ref):
    d = h_ref.shape[1]
    merged = None
    for n, br in enumerate((oa_ref, ob_ref, oc_ref, od_ref)):
        y = gt_ref[:, n * d:(n + 1) * d].astype(F32) * _dot(br[...], wb_ref[n])
        merged = y if merged is None else merged + y
    y = alpha * h_ref[...] + _dot(merged.astype(BF16), wo_ref[...])
    o_ref[...] = _layer_norm(y, g_ref[...], b_ref[...])


def _merge(outs, gates, h, w_branch, w_out, ln_g, ln_b, layer, alpha, tm=512):
    t, d = h.shape
    tm = min(tm, t)
    bw = outs[0].shape[1]
    once = pl.Buffered(1)
    return pl.pallas_call(
        functools.partial(_merge_body, alpha),
        grid=(t // tm,),
        in_specs=[pl.BlockSpec((tm, bw), lambda i: (i, 0))] * 4 + [
            pl.BlockSpec((tm, N_BRANCH * d), lambda i: (i, 0)),
            pl.BlockSpec((tm, d), lambda i: (i, 0)),
            pl.BlockSpec((None, N_BRANCH, bw, d), lambda i: (layer, 0, 0, 0), pipeline_mode=once),
            pl.BlockSpec((None, d, d), lambda i: (layer, 0, 0), pipeline_mode=once),
            pl.BlockSpec((None, None, 1, d), lambda i: (layer, 1, 0, 0)),
            pl.BlockSpec((None, None, 1, d), lambda i: (layer, 1, 0, 0)),
        ],
        out_specs=pl.BlockSpec((tm, d), lambda i: (i, 0)),
        out_shape=jax.ShapeDtypeStruct((t, d), F32),
        compiler_params=_cparams(("parallel",)),
        name="merge",
    )(*outs, gates, h, w_branch, w_out, ln_g, ln_b)


def _rope_lane_tables(positions, rot_dim, group):
    inv = ROPE_THETA ** (-jnp.arange(0, rot_dim, 2, dtype=F32) / rot_dim)
    ang = positions.astype(F32)[..., None] * inv
    cos, sin = jnp.cos(ang), jnp.sin(ang)
    half = rot_dim // 2
    rest = group - rot_dim
    shape = cos.shape[:-1]
    ones = jnp.ones(shape + (rest,), F32)
    c = jnp.concatenate([cos, cos, ones], axis=-1)
    sa = jnp.concatenate([-sin, jnp.zeros(shape + (half + rest,), F32)], axis=-1)
    sb = jnp.concatenate([jnp.zeros(shape + (half,), F32), sin, jnp.zeros(shape + (rest,), F32)], axis=-1)
    reps = LANES // group
    return tuple(jnp.tile(t, (1, 1, reps)) for t in (c, sa, sb))


def _pad_last(x, width):
    return jnp.pad(x, [(0, 0)] * (x.ndim - 1) + [(0, width - x.shape[-1])])


def _prepare_in_proj(w_in):
    w_in = w_in.astype(BF16)
    offs = [0]
    for sz in IN_SPLITS:
        offs.append(offs[-1] + sz)
    seg = lambda i: w_in[..., offs[i]:offs[i + 1]]
    (a_q, a_k, a_v, b_q, b_k, b_v, b_glow, b_r, c_q, c_kv, c_kr, d_z, d_xbc, d_dt, gate) = [seg(i) for i in range(15)]
    lead = w_in.shape[:-1]

    def pad_heads(w):
        return _pad_last(w.reshape(lead + (GLA_HEADS, GLA_DK)), LANES).reshape(lead + (GLA_HEADS * LANES,))

    d_x, d_bc = d_xbc[..., :SSD_INNER], d_xbc[..., SSD_INNER:]
    w_mix = jnp.concatenate([c_q, c_kv, a_q, a_k, a_v, pad_heads(b_q), pad_heads(b_k), b_v, b_r,
                             d_bc, d_x, d_z, _pad_last(c_kr, LANES)], axis=-1)
    w_small = _pad_last(jnp.concatenate([b_glow, d_dt], axis=-1), LANES)
    return w_mix, w_small, gate


def kernel(x, positions, ln_g, ln_b, ffn1_w_gu, ffn1_w_down, ffn2_w_gu, ffn2_w_down, w_in, b_gate, diff_lambda, diff_subln_g, gla_w_gate2, gla_b_gate, gla_norm_g, mla_q_norm_g, mla_w_uq, mla_kv_norm_g, mla_w_ukv, ssd_conv_w, ssd_conv_b, ssd_dt_bias, ssd_a_log, ssd_d, ssd_norm_g, w_branch, w_out):
    bsz, s_len, d = x.shape
    depth = ln_g.shape[0]
    alpha = (2 * depth) ** 0.25
    t = bsz * s_len

    rope_d = _rope_lane_tables(positions, DIFF_ROT, DIFF_HEAD_DIM)
    rope_m = _rope_lane_tables(positions, MLA_ROPE, LANES)

    w1_gu, w1_dn = ffn1_w_gu.astype(BF16), ffn1_w_down.astype(BF16)
    w2_gu, w2_dn = ffn2_w_gu.astype(BF16), ffn2_w_down.astype(BF16)
    w_mix, w_small, w_gate = _prepare_in_proj(w_in)
    b_gate3 = b_gate[:, None, :]
    ln_g4, ln_b4 = ln_g[:, :, None, :], ln_b[:, :, None, :]
    subln3 = diff_subln_g[:, None, :]
    wg2 = gla_w_gate2.reshape(depth, GLA_GATE_RANK, GLA_HEADS, GLA_DK).transpose(0, 2, 1, 3)
    wg2 = jnp.pad(wg2, ((0, 0), (0, 0), (SMALL_GLOW, LANES - GLA_GATE_RANK - SMALL_GLOW), (0, LANES - GLA_DK))).astype(BF16)
    bg2 = _pad_last(gla_b_gate.reshape(depth, GLA_HEADS, 1, GLA_DK), LANES)
    gla_ng3 = gla_norm_g[:, None, :]
    wq = mla_w_uq.reshape(depth, MLA_Q_RANK, MLA_HEADS, MLA_NOPE + MLA_ROPE).transpose(0, 2, 1, 3)
    wq = _pad_last(wq, 2 * LANES).astype(BF16)
    wkv = mla_w_ukv.reshape(depth, MLA_KV_RANK, MLA_HEADS, MLA_NOPE + MLA_V).transpose(0, 2, 1, 3).astype(BF16)
    mla_qg3, mla_kvg3 = mla_q_norm_g[:, None, :], mla_kv_norm_g[:, None, :]
    dtb3 = jnp.pad(ssd_dt_bias, ((0, 0), (SMALL_DT, LANES - SMALL_DT - SSD_HEADS)))[:, None, :]
    alog3 = jnp.pad(ssd_a_log, ((0, 0), (SMALL_DT, LANES - SMALL_DT - SSD_HEADS)))[:, None, :]
    dsk3 = jnp.repeat(ssd_d, SSD_HEADDIM, axis=-1)[:, None, :]
    cb3 = ssd_conv_b[:, None, :]
    ssd_ng3 = ssd_norm_g[:, None, :]
    wbr = w_branch.astype(BF16)
    wout = w_out.astype(BF16)

    h = x.reshape(t, d)
    for l in range(depth):
        h = _ffn(h, w1_gu, w1_dn, ln_g4, ln_b4, l, 0, alpha)
        p_mix, p_small, hb = _inproj(h, w_mix, w_small, l)
        gates = _gateproj(hb, w_gate, b_gate3, l)
        o_a = _diff_attention(p_mix, rope_d, diff_lambda, subln3, l, bsz, s_len)
        o_b = _gla(p_mix, p_small, wg2, bg2, gla_ng3, l, bsz, s_len)
        o_c = _mla_attention(p_mix, rope_m, mla_qg3, mla_kvg3, wq, wkv, l, bsz, s_len)
        o_d = _ssd(p_mix, p_small, ssd_conv_w, cb3, dtb3, alog3, dsk3, ssd_ng3, l, bsz, s_len)
        h = _merge((o_a, o_b, o_c, o_d), gates, h, wbr, wout, ln_g4, ln_b4, l, alpha)
        h = _ffn(h, w2_gu, w2_dn, ln_g4, ln_b4, l, 2, alpha)
    return h.reshape(bsz, s_len, d)
```

```python
import functools
import math

import jax
import jax.numpy as jnp
from jax import lax
from jax.experimental import pallas as pl
from jax.experimental.pallas import tpu as pltpu

F32 = jnp.float32
BF16 = jnp.bfloat16

D_MODEL = 2048
N_BRANCH = 4
BRANCH_WIDTH = D_MODEL // 4
ROPE_THETA = 500000.0
NORM_EPS = 1e-5

DIFF_HEADS = 4
DIFF_HEAD_DIM = 64
DIFF_ROT = 16

GLA_HEADS = 4
GLA_DK = 64
GLA_DV = 128
GLA_GATE_RANK = 16
GLA_TAU = 16.0
GLA_CHUNK = 64
GLA_UNROLL = 16

MLA_HEADS = 4
MLA_NOPE = 128
MLA_ROPE = 64
MLA_V = 128
MLA_Q_RANK = 384
MLA_KV_RANK = 128

SSD_HEADDIM = 64
SSD_INNER = 512
SSD_HEADS = 8
SSD_GROUPS = 2
SSD_STATE = 128
SSD_CONV = 4
SSD_CHUNK = 128

D_FF = 5632

IN_SPLITS = [512, 512, 512, 256, 256, 512, 16, 512, 384, 128, 64, 512, 1024, 8, 4 * D_MODEL]

LANES = 128
VMEM_LIMIT_BYTES = 63 * 2**20
FFN_TM = 1024
DIFF_HEADS_PER_STEP = 2
MLA_HEADS_PER_STEP = 4

MIX_CQ = 0
MIX_CKV = 3
MIX_AQ = 4
MIX_AK = 8
MIX_AV = 12
MIX_BQ = 16
MIX_BK = 20
MIX_BV = 24
MIX_BR = 28
MIX_SBC = 32
MIX_SX = 36
MIX_SZ = 40
MIX_CKR = 44
MIX_WIDTH = 45 * LANES
SMALL_GLOW = 0
SMALL_DT = 16

NEG_BIG = -1e30
LOG2E = math.log2(math.e)


def _cparams(semantics):
    return pltpu.CompilerParams(dimension_semantics=semantics, vmem_limit_bytes=VMEM_LIMIT_BYTES)


def _layer_norm(y, g, b):
    mu = jnp.mean(y, axis=-1, keepdims=True)
    yc = y - mu
    var = jnp.mean(yc * yc, axis=-1, keepdims=True)
    return yc * lax.rsqrt(var + NORM_EPS) * g + b


def _rms(x, g):
    return x * lax.rsqrt(jnp.mean(x * x, axis=-1, keepdims=True) + NORM_EPS) * g


def _sigmoid(x):
    return 1.0 / (1.0 + jnp.exp(-x))


def _silu(x):
    return x * _sigmoid(x)


def _softplus(x):
    return jnp.maximum(x, 0.0) + jnp.log(1.0 + jnp.exp(-jnp.abs(x)))


def _dot(a, b):
    return jnp.dot(a, b, preferred_element_type=F32)


def _dot_nt(a, b):
    return lax.dot_general(a, b, (((1,), (1,)), ((), ())), preferred_element_type=F32)


def _dot_tn(a, b):
    return lax.dot_general(a, b, (((0,), (0,)), ((), ())), preferred_element_type=F32)


def _ffn_body(alpha, nj, h_ref, wg_ref, wu_ref, wd_ref, g_ref, b_ref, o_ref, hb_ref):
    j = pl.program_id(1)

    def chunk_product():
        hb = hb_ref[...]
        gate = _dot(hb, wg_ref[...])
        up = _dot(hb, wu_ref[...])
        act = (_silu(gate) * up).astype(BF16)
        return _dot(act, wd_ref[...])

    @pl.when(j == 0)
    def _():
        h = h_ref[...]
        hb_ref[...] = h.astype(BF16)
        o_ref[...] = (2.0 * alpha) * h + chunk_product()

    @pl.when(j > 0)
    def _():
        o_ref[...] += chunk_product()

    @pl.when(j == nj - 1)
    def _():
        o_ref[...] = _layer_norm(0.5 * o_ref[...], g_ref[...], b_ref[...])


def _ffn(h, w_gu, w_down, ln_g, ln_b, layer, ln_idx, alpha, tm=FFN_TM, tf=512):
    t, d = h.shape
    dff = w_down.shape[1]
    nj = dff // tf
    tm = min(tm, t)
    grid = (t // tm, nj)
    return pl.pallas_call(
        functools.partial(_ffn_body, alpha, nj),
        grid=grid,
        in_specs=[
            pl.BlockSpec((tm, d), lambda i, j: (i, 0)),
            pl.BlockSpec((None, d, tf), lambda i, j: (layer, 0, j)),
            pl.BlockSpec((None, d, tf), lambda i, j: (layer, 0, nj + j)),
            pl.BlockSpec((None, tf, d), lambda i, j: (layer, j, 0)),
            pl.BlockSpec((None, None, 1, d), lambda i, j: (layer, ln_idx, 0, 0)),
            pl.BlockSpec((None, None, 1, d), lambda i, j: (layer, ln_idx, 0, 0)),
        ],
        out_specs=pl.BlockSpec((tm, d), lambda i, j: (i, 0)),
        out_shape=jax.ShapeDtypeStruct((t, d), F32),
        scratch_shapes=[pltpu.VMEM((tm, d), BF16)],
        compiler_params=_cparams(("parallel", "arbitrary")),
        name="ffn",
    )(h, w_gu, w_gu, w_down, ln_g, ln_b)


def _inproj_body(h_ref, w_ref, ws_ref, p_ref, ps_ref, hb_ref):
    n = pl.program_id(1)

    @pl.when(n == 0)
    def _():
        hb = h_ref[...].astype(BF16)
        hb_ref[...] = hb
        ps_ref[...] = _dot(hb, ws_ref[...])
        p_ref[...] = _dot(hb, w_ref[...]).astype(BF16)

    @pl.when(n > 0)
    def _():
        p_ref[...] = _dot(hb_ref[...], w_ref[...]).astype(BF16)


def _inproj(h, w_mix, w_small, layer, tm=1024, tn=1920):
    t, d = h.shape
    n = w_mix.shape[2]
    tm = min(tm, t)
    return pl.pallas_call(
        _inproj_body,
        grid=(t // tm, n // tn),
        in_specs=[
            pl.BlockSpec((tm, d), lambda i, j: (i, 0)),
            pl.BlockSpec((None, d, tn), lambda i, j: (layer, 0, j)),
            pl.BlockSpec((None, d, LANES), lambda i, j: (layer, 0, 0)),
        ],
        out_specs=[
            pl.BlockSpec((tm, tn), lambda i, j: (i, j)),
            pl.BlockSpec((tm, LANES), lambda i, j: (i, 0)),
            pl.BlockSpec((tm, d), lambda i, j: (i, 0)),
        ],
        out_shape=[jax.ShapeDtypeStruct((t, n), BF16), jax.ShapeDtypeStruct((t, LANES), F32),
                   jax.ShapeDtypeStruct((t, d), BF16)],
        compiler_params=_cparams(("parallel", "arbitrary")),
        name="inproj",
    )(h, w_mix, w_small)


def _gate_body(hb_ref, w_ref, b_ref, o_ref):
    z = _dot(hb_ref[...], w_ref[...]) + b_ref[...]
    o_ref[...] = _sigmoid(z).astype(BF16)


def _gateproj(hb, w_gate, b_gate, layer, tm=1024, tn=2048):
    t, d = hb.shape
    n = w_gate.shape[2]
    tm = min(tm, t)
    return pl.pallas_call(
        _gate_body,
        grid=(t // tm, n // tn),
        in_specs=[
            pl.BlockSpec((tm, d), lambda i, j: (i, 0)),
            pl.BlockSpec((None, d, tn), lambda i, j: (layer, 0, j)),
            pl.BlockSpec((None, 1, tn), lambda i, j: (layer, 0, j)),
        ],
        out_specs=pl.BlockSpec((tm, tn), lambda i, j: (i, j)),
        out_shape=jax.ShapeDtypeStruct((t, n), BF16),
        compiler_params=_cparams(("parallel", "arbitrary")),
        name="gateproj",
    )(hb, w_gate, b_gate)


def _rope(x, c, sa, sb, half):
    return x * c + pltpu.roll(x, LANES - half, 1) * sa + pltpu.roll(x, half, 1) * sb


def _causal_block(q_rows, k, v, blk, tq, n_maps):
    s = _dot_nt(q_rows, k)
    rows = lax.broadcasted_iota(jnp.int32, s.shape, 0)
    if n_maps > 1:
        rows = jnp.where(rows >= tq, rows - tq, rows)
    cols = lax.broadcasted_iota(jnp.int32, s.shape, 1)
    s = jnp.where(cols <= rows + blk * tq, s, NEG_BIG)
    m = jnp.max(s, axis=-1, keepdims=True)
    p = jnp.exp2(s - m)
    l = jnp.sum(p, axis=-1, keepdims=True)
    return _dot(p.astype(BF16), v) / l


def _diff_body(lam_init, tq, q_ref, k_ref, v_ref, c_ref, sa_ref, sb_ref, lam_ref, g_ref, o_ref, qs_ref, ks_ref):
    s_len = q_ref.shape[0]
    c, sa, sb = c_ref[...], sa_ref[...], sb_ref[...]
    first = lax.broadcasted_iota(jnp.int32, (1, LANES), 1) < DIFF_HEAD_DIM
    lp = lam_ref[...]
    lam = (jnp.exp(jnp.sum(lp[0:1] * lp[1:2], axis=-1, keepdims=True))
           - jnp.exp(jnp.sum(lp[2:3] * lp[3:4], axis=-1, keepdims=True)) + lam_init)
    g = g_ref[...]
    for hh in range(DIFF_HEADS_PER_STEP):
        lanes = slice(hh * LANES, (hh + 1) * LANES)
        q = _rope(q_ref[:, lanes].astype(F32), c, sa, sb, DIFF_ROT // 2) * (DIFF_HEAD_DIM ** -0.5 * LOG2E)
        k = _rope(k_ref[:, lanes].astype(F32), c, sa, sb, DIFF_ROT // 2)
        qs_ref[2 * hh] = jnp.where(first, q, 0.0).astype(BF16)
        qs_ref[2 * hh + 1] = jnp.where(first, 0.0, q).astype(BF16)
        ks_ref[hh] = k.astype(BF16)
    for blk in reversed(range(s_len // tq)):
        r0, length = blk * tq, (blk + 1) * tq
        for hh in range(DIFF_HEADS_PER_STEP):
            lanes = slice(hh * LANES, (hh + 1) * LANES)
            q_rows = jnp.concatenate([qs_ref[2 * hh, r0:r0 + tq, :], qs_ref[2 * hh + 1, r0:r0 + tq, :]], axis=0)
            o = _causal_block(q_rows, ks_ref[hh, 0:length, :], v_ref[0:length, lanes], blk, tq, 2)
            od = o[:tq] - lam * o[tq:]
            o_ref[r0:r0 + tq, lanes] = (_rms(od, g) * (1.0 - lam_init)).astype(BF16)


def _diff_attention(p_mix, rope_d, lam, subln_g, layer, bsz, s_len, tq=256):
    t = p_mix.shape[0]
    lam_init = 0.8 - 0.6 * math.exp(-0.3 * layer)
    tq = min(tq, s_len)
    hp = DIFF_HEADS_PER_STEP
    blk = lambda off: pl.BlockSpec((s_len, hp * LANES), lambda b, h: (b, off // hp + h))
    tab = pl.BlockSpec((None, s_len, LANES), lambda b, h: (b, 0, 0))
    return pl.pallas_call(
        functools.partial(_diff_body, lam_init, tq),
        grid=(bsz, DIFF_HEADS // hp),
        in_specs=[blk(MIX_AQ), blk(MIX_AK), blk(MIX_AV), tab, tab, tab,
                  pl.BlockSpec((None, 4, DIFF_HEAD_DIM), lambda b, h: (layer, 0, 0)),
                  pl.BlockSpec((None, 1, LANES), lambda b, h: (layer, 0, 0))],
        out_specs=pl.BlockSpec((s_len, hp * LANES), lambda b, h: (b, h)),
        out_shape=jax.ShapeDtypeStruct((t, BRANCH_WIDTH), BF16),
        scratch_shapes=[pltpu.VMEM((2 * hp, s_len, LANES), BF16), pltpu.VMEM((hp, s_len, LANES), BF16)],
        compiler_params=_cparams(("parallel", "arbitrary")),
        name="diff_attn",
    )(p_mix, p_mix, p_mix, *rope_d, lam, subln_g)


def _mla_body(tq, cq_ref, ckv_ref, kr_ref, c_ref, sa_ref, sb_ref, qg_ref, kvg_ref, wq_ref, wkv_ref,
              o_ref, qs_ref, ks_ref, vs_ref):
    s_len = cq_ref.shape[0]
    c, sa, sb = c_ref[...], sa_ref[...], sb_ref[...]
    scale = (MLA_NOPE + MLA_ROPE) ** -0.5 * LOG2E
    cqn = _rms(cq_ref[...].astype(F32), qg_ref[...]).astype(BF16)
    ckvn = _rms(ckv_ref[...].astype(F32), kvg_ref[...]).astype(BF16)
    k_rope = _rope(kr_ref[...].astype(F32), c, sa, sb, MLA_ROPE // 2).astype(BF16)
    for hh in range(MLA_HEADS_PER_STEP):
        qc = _dot(cqn, wq_ref[hh])
        qs_ref[hh, :, 0:LANES] = (qc[:, 0:LANES] * scale).astype(BF16)
        qs_ref[hh, :, LANES:] = (_rope(qc[:, LANES:], c, sa, sb, MLA_ROPE // 2) * scale).astype(BF16)
        kv = _dot(ckvn, wkv_ref[hh])
        ks_ref[hh, :, 0:LANES] = kv[:, 0:LANES].astype(BF16)
        ks_ref[hh, :, LANES:] = k_rope
        vs_ref[hh] = kv[:, LANES:].astype(BF16)
    for blk in reversed(range(s_len // tq)):
        r0, length = blk * tq, (blk + 1) * tq
        for hh in range(MLA_HEADS_PER_STEP):
            o = _causal_block(qs_ref[hh, r0:r0 + tq, :], ks_ref[hh, 0:length, :], vs_ref[hh, 0:length, :], blk, tq, 1)
            o_ref[r0:r0 + tq, hh * LANES:(hh + 1) * LANES] = o.astype(BF16)


def _mla_attention(p_mix, rope_m, q_norm_g, kv_norm_g, w_uq, w_ukv, layer, bsz, s_len, tq=512):
    t = p_mix.shape[0]
    tq = min(tq, s_len)
    hp = MLA_HEADS_PER_STEP
    tab = pl.BlockSpec((None, s_len, LANES), lambda b, h: (b, 0, 0))
    return pl.pallas_call(
        functools.partial(_mla_body, tq),
        grid=(bsz, MLA_HEADS // hp),
        in_specs=[pl.BlockSpec((s_len, MLA_Q_RANK), lambda b, h: (b, MIX_CQ)),
                  pl.BlockSpec((s_len, LANES), lambda b, h: (b, MIX_CKV)),
                  pl.BlockSpec((s_len, LANES), lambda b, h: (b, MIX_CKR)),
                  tab, tab, tab,
                  pl.BlockSpec((None, 1, MLA_Q_RANK), lambda b, h: (layer, 0, 0)),
                  pl.BlockSpec((None, 1, MLA_KV_RANK), lambda b, h: (layer, 0, 0)),
                  pl.BlockSpec((None, hp, MLA_Q_RANK, 2 * LANES), lambda b, h: (layer, h, 0, 0)),
                  pl.BlockSpec((None, hp, MLA_KV_RANK, 2 * LANES), lambda b, h: (layer, h, 0, 0))],
        out_specs=pl.BlockSpec((s_len, hp * LANES), lambda b, h: (b, h)),
        out_shape=jax.ShapeDtypeStruct((t, BRANCH_WIDTH), BF16),
        scratch_shapes=[pltpu.VMEM((hp, s_len, 2 * LANES), BF16), pltpu.VMEM((hp, s_len, 2 * LANES), BF16),
                        pltpu.VMEM((hp, s_len, LANES), BF16)],
        compiler_params=_cparams(("parallel", "arbitrary")),
        name="mla_attn",
    )(p_mix, p_mix, p_mix, *rope_m, q_norm_g, kv_norm_g, w_uq, w_ukv)


def _chunk_cumsum(x, chunk):
    pos = lax.broadcasted_iota(jnp.int32, x.shape, 0) % chunk
    step = 1
    while step < chunk:
        x = x + jnp.where(pos >= step, pltpu.roll(x, step, 0), 0.0)
        step *= 2
    return x


def _gla_body(q_ref, k_ref, v_ref, r_ref, ps_ref, wg_ref, bg_ref, ng_ref, o_ref, b_ref):
    s_len = q_ref.shape[0]
    ck = GLA_CHUNK
    x = _dot(ps_ref[...].astype(BF16), wg_ref[...]) + bg_ref[...]
    g = (jnp.minimum(x, 0.0) - jnp.log(1.0 + jnp.exp(-jnp.abs(x)))) / GLA_TAU
    b_ref[...] = _chunk_cumsum(g, ck)
    tril = (lax.broadcasted_iota(jnp.int32, (ck, ck), 0) >= lax.broadcasted_iota(jnp.int32, (ck, ck), 1))
    ng = ng_ref[...]

    def chunk(ci, st):
        sl = pl.ds(pl.multiple_of(ci * ck, ck), ck)
        bc = b_ref[sl, :]
        bl = b_ref[pl.ds(ci * ck + ck - 1, 1), :]
        kc = k_ref[sl, :].astype(F32)
        qt = (q_ref[sl, :].astype(F32) * (GLA_DK ** -0.5) * jnp.exp(bc)).astype(BF16)
        kt = (kc * jnp.exp(-bc)).astype(BF16)
        kd = (kc * jnp.exp(bl - bc)).astype(BF16)
        vc = v_ref[sl, :]
        att = jnp.where(tril, _dot_nt(qt, kt), 0.0).astype(BF16)
        o = _dot(att, vc) + _dot_nt(qt, st.astype(BF16))
        o_ref[sl, :] = (_rms(o, ng) * _silu(r_ref[sl, :].astype(F32))).astype(BF16)
        return jnp.exp(bl) * st + _dot_tn(vc, kd)

    n_chunks = s_len // ck
    lax.fori_loop(0, n_chunks, chunk, jnp.zeros((GLA_DV, LANES), F32), unroll=math.gcd(n_chunks, GLA_UNROLL))


def _gla(p_mix, p_small, w_gate2, b_gate, norm_g, layer, bsz, s_len):
    t = p_mix.shape[0]
    blk = lambda off: pl.BlockSpec((s_len, LANES), lambda b, h: (b, off + h))
    return pl.pallas_call(
        _gla_body,
        grid=(bsz, GLA_HEADS),
        in_specs=[blk(MIX_BQ), blk(MIX_BK), blk(MIX_BV), blk(MIX_BR),
                  pl.BlockSpec((s_len, LANES), lambda b, h: (b, 0)),
                  pl.BlockSpec((None, None, LANES, LANES), lambda b, h: (layer, h, 0, 0)),
                  pl.BlockSpec((None, None, 1, LANES), lambda b, h: (layer, h, 0, 0)),
                  pl.BlockSpec((None, 1, LANES), lambda b, h: (layer, 0, 0))],
        out_specs=pl.BlockSpec((s_len, LANES), lambda b, h: (b, h)),
        out_shape=jax.ShapeDtypeStruct((t, BRANCH_WIDTH), BF16),
        scratch_shapes=[pltpu.VMEM((s_len, LANES), F32)],
        compiler_params=_cparams(("parallel", "arbitrary")),
        name="gla",
    )(p_mix, p_mix, p_mix, p_mix, p_small, w_gate2, b_gate, norm_g)


def _causal_conv_silu(x, w, b):
    row = lax.broadcasted_iota(jnp.int32, x.shape, 0)
    y = x * w[SSD_CONV - 1:SSD_CONV] + b
    for shift in range(1, SSD_CONV):
        xs = jnp.where(row >= shift, pltpu.roll(x, shift, 0), 0.0)
        y = y + xs * w[SSD_CONV - 1 - shift:SSD_CONV - shift]
    return _silu(y)


def _lane_col(a, idx, rows):
    return jnp.broadcast_to(a[:, idx:idx + 1], (rows, LANES))


def _ssd_body(bc_ref, x_ref, z_ref, ps_ref, cw_bc_ref, cw_x_ref, cb_bc_ref, cb_x_ref, dtb_ref, alog_ref,
              dsk_ref, ng_ref, o_ref, xs_ref, bs_ref, cs_ref, dt_ref, ac_ref):
    s_len = x_ref.shape[0]
    ck = SSD_CHUNK
    n_pairs = SSD_HEADS // 2
    xs_ref[...] = _causal_conv_silu(x_ref[...].astype(F32), cw_x_ref[...], cb_x_ref[...])
    bcv = _causal_conv_silu(bc_ref[...].astype(F32), cw_bc_ref[...], cb_bc_ref[...])
    bs_ref[...] = bcv[:, 0:SSD_GROUPS * SSD_STATE].astype(BF16)
    cs_ref[...] = bcv[:, SSD_GROUPS * SSD_STATE:].astype(BF16)
    dt = _softplus(ps_ref[...] + dtb_ref[...])
    dt_ref[...] = dt
    ac_ref[...] = _chunk_cumsum(dt * (-jnp.exp(alog_ref[...])), ck)

    tril = (lax.broadcasted_iota(jnp.int32, (ck, ck), 0) >= lax.broadcasted_iota(jnp.int32, (ck, ck), 1))
    first = lax.broadcasted_iota(jnp.int32, (1, LANES), 1) < SSD_HEADDIM

    def chunk(ci, states):
        sl = pl.ds(pl.multiple_of(ci * ck, ck), ck)
        new_states = []
        a = ac_ref[sl, :]
        a_t = a.T
        a_last = ac_ref[pl.ds(ci * ck + ck - 1, 1), :]
        dtc = dt_ref[sl, :]
        for grp in range(SSD_GROUPS):
            bm = bs_ref[sl, grp * SSD_STATE:(grp + 1) * SSD_STATE]
            cm = cs_ref[sl, grp * SSD_STATE:(grp + 1) * SSD_STATE]
            cb = _dot_nt(cm, bm)
            ys = []
            for pp in range(n_pairs // SSD_GROUPS):
                pair = grp * (n_pairs // SSD_GROUPS) + pp
                h0 = SMALL_DT + 2 * pair
                a0, a1 = _lane_col(a, h0, ck), _lane_col(a, h0 + 1, ck)
                a_pair = jnp.where(first, a0, a1)
                dt_pair = jnp.where(first, _lane_col(dtc, h0, ck), _lane_col(dtc, h0 + 1, ck))
                al_pair = jnp.where(first, _lane_col(a_last, h0, 1), _lane_col(a_last, h0 + 1, 1))
                xp = xs_ref[sl, pair * LANES:(pair + 1) * LANES]
                xdt = xp * dt_pair
                m0 = (cb * jnp.where(tril, jnp.exp(a0 - a_t[h0:h0 + 1, :]), 0.0)).astype(BF16)
                m1 = (cb * jnp.where(tril, jnp.exp(a1 - a_t[h0 + 1:h0 + 2, :]), 0.0)).astype(BF16)
                y = (_dot(m0, jnp.where(first, xdt, 0.0).astype(BF16))
                     + _dot(m1, jnp.where(first, 0.0, xdt).astype(BF16)))
                st = states[pair]
                y = y + _dot(cm, st.astype(BF16)) * jnp.exp(a_pair)
                new_states.append(jnp.exp(al_pair) * st + _dot_tn(bm, (xdt * jnp.exp(al_pair - a_pair)).astype(BF16)))
                ys.append(y + dsk_ref[:, pair * LANES:(pair + 1) * LANES] * xp)
            width = SSD_INNER // SSD_GROUPS
            lo = grp * width
            yg = jnp.concatenate(ys, axis=1) * _silu(z_ref[sl, lo:lo + width].astype(F32))
            o_ref[sl, lo:lo + width] = _rms(yg, ng_ref[:, lo:lo + width]).astype(BF16)
        return tuple(new_states)

    n_chunks = s_len // ck
    zero = jnp.zeros((SSD_STATE, LANES), F32)
    lax.fori_loop(0, n_chunks, chunk, (zero,) * n_pairs, unroll=math.gcd(n_chunks, 2))


def _ssd(p_mix, p_small, conv_w, conv_b, dt_bias, a_log, d_skip, norm_g, layer, bsz, s_len):
    t = p_mix.shape[0]
    wide = SSD_INNER
    row = lambda width, idx: pl.BlockSpec((s_len, width), lambda b: (b, idx))
    par = lambda rows, width, idx: pl.BlockSpec((None, rows, width), lambda b: (layer, 0, idx))
    return pl.pallas_call(
        _ssd_body,
        grid=(bsz,),
        in_specs=[row(wide, MIX_SBC * LANES // wide), row(wide, MIX_SX * LANES // wide),
                  row(wide, MIX_SZ * LANES // wide), row(LANES, 0),
                  par(SSD_CONV, wide, 1), par(SSD_CONV, wide, 0), par(1, wide, 1), par(1, wide, 0),
                  par(1, LANES, 0), par(1, LANES, 0), par(1, wide, 0), par(1, wide, 0)],
        out_specs=pl.BlockSpec((s_len, wide), lambda b: (b, 0)),
        out_shape=jax.ShapeDtypeStruct((t, wide), BF16),
        scratch_shapes=[pltpu.VMEM((s_len, wide), F32), pltpu.VMEM((s_len, SSD_GROUPS * SSD_STATE), BF16),
                        pltpu.VMEM((s_len, SSD_GROUPS * SSD_STATE), BF16), pltpu.VMEM((s_len, LANES), F32),
                        pltpu.VMEM((s_len, LANES), F32)],
        compiler_params=_cparams(("parallel",)),
        name="ssd",
    )(p_mix, p_mix, p_mix, p_small, conv_w, conv_w, conv_b, conv_b, dt_bias, a_log, d_skip, norm_g)


def _merge_body(alpha, oa_ref, ob_ref, oc_ref, od_ref, gt_ref, h_ref, wb_ref, wo_ref, g_ref, b_ref, o_ref):
    d = h_ref.shape[1]
    merged = None
    for n, br in enumerate((oa_ref, ob_ref, oc_ref, od_ref)):
        y = gt_ref[:, n * d:(n + 1) * d].astype(F32) * _dot(br[...], wb_ref[n])
        merged = y if merged is None else merged + y
    y = alpha * h_ref[...] + _dot(merged.astype(BF16), wo_ref[...])
    o_ref[...] = _layer_norm(y, g_ref[...], b_ref[...])


def _merge(outs, gates, h, w_branch, w_out, ln_g, ln_b, layer, alpha, tm=512):
    t, d = h.shape
    tm = min(tm, t)
    bw = outs[0].shape[1]
    once = pl.Buffered(1)
    return pl.pallas_call(
        functools.partial(_merge_body, alpha),
        grid=(t // tm,),
        in_specs=[pl.BlockSpec((tm, bw), lambda i: (i, 0))] * 4 + [
            pl.BlockSpec((tm, N_BRANCH * d), lambda i: (i, 0)),
            pl.BlockSpec((tm, d), lambda i: (i, 0)),
            pl.BlockSpec((None, N_BRANCH, bw, d), lambda i: (layer, 0, 0, 0), pipeline_mode=once),
            pl.BlockSpec((None, d, d), lambda i: (layer, 0, 0), pipeline_mode=once),
            pl.BlockSpec((None, None, 1, d), lambda i: (layer, 1, 0, 0)),
            pl.BlockSpec((None, None, 1, d), lambda i: (layer, 1, 0, 0)),
        ],
        out_specs=pl.BlockSpec((tm, d), lambda i: (i, 0)),
        out_shape=jax.ShapeDtypeStruct((t, d), F32),
        compiler_params=_cparams(("parallel",)),
        name="merge",
    )(*outs, gates, h, w_branch, w_out, ln_g, ln_b)


def _rope_lane_tables(positions, rot_dim, group):
    inv = ROPE_THETA ** (-jnp.arange(0, rot_dim, 2, dtype=F32) / rot_dim)
    ang = positions.astype(F32)[..., None] * inv
    cos, sin = jnp.cos(ang), jnp.sin(ang)
    half = rot_dim // 2
    rest = group - rot_dim
    shape = cos.shape[:-1]
    ones = jnp.ones(shape + (rest,), F32)
    c = jnp.concatenate([cos, cos, ones], axis=-1)
    sa = jnp.concatenate([-sin, jnp.zeros(shape + (half + rest,), F32)], axis=-1)
    sb = jnp.concatenate([jnp.zeros(shape + (half,), F32), sin, jnp.zeros(shape + (rest,), F32)], axis=-1)
    reps = LANES // group
    return tuple(jnp.tile(t, (1, 1, reps)) for t in (c, sa, sb))


def _pad_last(x, width):
    return jnp.pad(x, [(0, 0)] * (x.ndim - 1) + [(0, width - x.shape[-1])])


def _prepare_in_proj(w_in):
    w_in = w_in.astype(BF16)
    offs = [0]
    for sz in IN_SPLITS:
        offs.append(offs[-1] + sz)
    seg = lambda i: w_in[..., offs[i]:offs[i + 1]]
    (a_q, a_k, a_v, b_q, b_k, b_v, b_glow, b_r, c_q, c_kv, c_kr, d_z, d_xbc, d_dt, gate) = [seg(i) for i in range(15)]
    lead = w_in.shape[:-1]

    def pad_heads(w):
        return _pad_last(w.reshape(lead + (GLA_HEADS, GLA_DK)), LANES).reshape(lead + (GLA_HEADS * LANES,))

    d_x, d_bc = d_xbc[..., :SSD_INNER], d_xbc[..., SSD_INNER:]
    w_mix = jnp.concatenate([c_q, c_kv, a_q, a_k, a_v, pad_heads(b_q), pad_heads(b_k), b_v, b_r,
                             d_bc, d_x, d_z, _pad_last(c_kr, LANES)], axis=-1)
    w_small = _pad_last(jnp.concatenate([b_glow, d_dt], axis=-1), LANES)
    return w_mix, w_small, gate


def kernel(x, positions, ln_g, ln_b, ffn1_w_gu, ffn1_w_down, ffn2_w_gu, ffn2_w_down, w_in, b_gate, diff_lambda, diff_subln_g, gla_w_gate2, gla_b_gate, gla_norm_g, mla_q_norm_g, mla_w_uq, mla_kv_norm_g, mla_w_ukv, ssd_conv_w, ssd_conv_b, ssd_dt_bias, ssd_a_log, ssd_d, ssd_norm_g, w_branch, w_out):
    bsz, s_len, d = x.shape
    depth = ln_g.shape[0]
    alpha = (2 * depth) ** 0.25
    t = bsz * s_len

    rope_d = _rope_lane_tables(positions, DIFF_ROT, DIFF_HEAD_DIM)
    rope_m = _rope_lane_tables(positions, MLA_ROPE, LANES)

    w1_gu, w1_dn = ffn1_w_gu.astype(BF16), ffn1_w_down.astype(BF16)
    w2_gu, w2_dn = ffn2_w_gu.astype(BF16), ffn2_w_down.astype(BF16)
    w_mix, w_small, w_gate = _prepare_in_proj(w_in)
    b_gate3 = b_gate[:, None, :]
    ln_g4, ln_b4 = ln_g[:, :, None, :], ln_b[:, :, None, :]
    subln3 = diff_subln_g[:, None, :]
    wg2 = gla_w_gate2.reshape(depth, GLA_GATE_RANK, GLA_HEADS, GLA_DK).transpose(0, 2, 1, 3)
    wg2 = jnp.pad(wg2, ((0, 0), (0, 0), (SMALL_GLOW, LANES - GLA_GATE_RANK - SMALL_GLOW), (0, LANES - GLA_DK))).astype(BF16)
    bg2 = _pad_last(gla_b_gate.reshape(depth, GLA_HEADS, 1, GLA_DK), LANES)
    gla_ng3 = gla_norm_g[:, None, :]
    wq = mla_w_uq.reshape(depth, MLA_Q_RANK, MLA_HEADS, MLA_NOPE + MLA_ROPE).transpose(0, 2, 1, 3)
    wq = _pad_last(wq, 2 * LANES).astype(BF16)
    wkv = mla_w_ukv.reshape(depth, MLA_KV_RANK, MLA_HEADS, MLA_NOPE + MLA_V).transpose(0, 2, 1, 3).astype(BF16)
    mla_qg3, mla_kvg3 = mla_q_norm_g[:, None, :], mla_kv_norm_g[:, None, :]
    dtb3 = jnp.pad(ssd_dt_bias, ((0, 0), (SMALL_DT, LANES - SMALL_DT - SSD_HEADS)))[:, None, :]
    alog3 = jnp.pad(ssd_a_log, ((0, 0), (SMALL_DT, LANES - SMALL_DT - SSD_HEADS)))[:, None, :]
    dsk3 = jnp.repeat(ssd_d, SSD_HEADDIM, axis=-1)[:, None, :]
    cb3 = ssd_conv_b[:, None, :]
    ssd_ng3 = ssd_norm_g[:, None, :]
    wbr = w_branch.astype(BF16)
    wout = w_out.astype(BF16)

    h = x.reshape(t, d)
    for l in range(depth):
        h = _ffn(h, w1_gu, w1_dn, ln_g4, ln_b4, l, 0, alpha)
        p_mix, p_small, hb = _inproj(h, w_mix, w_small, l)
        gates = _gateproj(hb, w_gate, b_gate3, l)
        o_a = _diff_attention(p_mix, rope_d, diff_lambda, subln3, l, bsz, s_len)
        o_b = _gla(p_mix, p_small, wg2, bg2, gla_ng3, l, bsz, s_len)
        o_c = _mla_attention(p_mix, rope_m, mla_qg3, mla_kvg3, wq, wkv, l, bsz, s_len)
        o_d = _ssd(p_mix, p_small, ssd_conv_w, cb3, dtb3, alog3, dsk3, ssd_ng3, l, bsz, s_len)
        h = _merge((o_a, o_b, o_c, o_d), gates, h, wbr, wout, ln_g4, ln_b4, l, alpha)
        h = _ffn(h, w2_gu, w2_dn, ln_g4, ln_b4, l, 2, alpha)
    return h.reshape(bsz, s_len, d)
```
